```python
import jax, jax.numpy as jnp
from jax import lax
import numpy as np

D_MODEL = 2048
BATCH = 4
SEQ = 2048
DEPTH = 1
DEC_BATCH = 16
DEC_SEQ = 64
PAST_LEN = 1024

CHUNK = 64
N_RET_HEADS = 8
RET_QK_DIM = 128
RET_V_DIM = 256
RET_QK_WIDTH = N_RET_HEADS * RET_QK_DIM
RET_V_WIDTH = N_RET_HEADS * RET_V_DIM
CONV_CH = 1024
CONV_WIDTH = 31
N_EXPERTS = 32
TOP_K = 4
D_EXPERT = 2048
SWIGLU_LIMIT = 7.0
SWIGLU_ALPHA = 1.702
MOE_BLOCK = 256
ROPE_BASE = 10000.0
RMS_EPS = 1e-6
LN_EPS = 1e-5
IN_WIDTH = 2 * RET_QK_WIDTH + 2 * RET_V_WIDTH + 2 * CONV_CH + 2 * D_MODEL

kernel_name = 'retnet_conformer_gated_moe_stream_step'


def rms_norm(x, g):
    x32 = x.astype(jnp.float32)
    y = x32 * lax.rsqrt(jnp.mean(x32 * x32, axis=-1, keepdims=True) + RMS_EPS)
    return (y * g.astype(jnp.float32)).astype(x.dtype)


def layer_norm(x, g, b):
    x32 = x.astype(jnp.float32)
    mu = jnp.mean(x32, axis=-1, keepdims=True)
    var = jnp.mean(jnp.square(x32 - mu), axis=-1, keepdims=True)
    y = (x32 - mu) * lax.rsqrt(var + LN_EPS) * g.astype(jnp.float32) + b.astype(jnp.float32)
    return y.astype(x.dtype)


def rope(x, pos):
    half = x.shape[-1] // 2
    freq = ROPE_BASE ** (-jnp.arange(half, dtype=jnp.float32) / half)
    ang = pos.astype(jnp.float32)[:, None] * freq[None, :]
    cos = jnp.cos(ang)[None, :, None, :]
    sin = jnp.sin(ang)[None, :, None, :]
    x = x.astype(jnp.float32)
    x1, x2 = x[..., :half], x[..., half:]
    return jnp.concatenate([x1 * cos - x2 * sin, x1 * sin + x2 * cos], axis=-1)


def retention_block(state, q, k, v, log_gamma):
    L = q.shape[1]
    pos = jnp.arange(L, dtype=jnp.float32)
    rel = pos[:, None] - pos[None, :]
    decay = jnp.where(rel >= 0, jnp.exp(log_gamma[:, None, None] * jnp.maximum(rel, 0.0)), 0.0)
    scores = jnp.einsum('blhd,bmhd->bhlm', q, k) * decay[None]
    intra = jnp.einsum('bhlm,bmhe->blhe', scores, v)
    cross = jnp.einsum('blhd,bhde->blhe', q, state) * jnp.exp(log_gamma[None, :] * (pos[:, None] + 1.0))[None, :, :, None]
    k_dec = k * jnp.exp(log_gamma[None, :] * (L - 1.0 - pos[:, None]))[None, :, :, None]
    new_state = jnp.exp(log_gamma * L)[None, :, None, None] * state + jnp.einsum('blhd,blhe->bhde', k_dec, v)
    return new_state, intra + cross


def retention(state, q, k, v):
    B, T, H, _ = q.shape
    log_gamma = jnp.log1p(-jnp.exp2(-5.0 - jnp.arange(H, dtype=jnp.float32)))
    if T <= CHUNK:
        return retention_block(state, q, k, v, log_gamma)
    nc = T // CHUNK

    def to_chunks(a):
        return jnp.moveaxis(a.reshape((B, nc, CHUNK) + a.shape[2:]), 1, 0)

    def step(s, qkv):
        qc, kc, vc = qkv
        return retention_block(s, qc, kc, vc, log_gamma)

    s_final, out = lax.scan(step, state, (to_chunks(q), to_chunks(k), to_chunks(v)))
    out = jnp.moveaxis(out, 0, 1).reshape(B, T, H, out.shape[-1])
    return s_final, out


def causal_depthwise_conv(hist, x, w, b):
    full = jnp.concatenate([hist.astype(x.dtype), x], axis=1)
    y = lax.conv_general_dilated(full, w[:, None, :].astype(x.dtype), (1,), 'VALID',
                                 dimension_numbers=('NWC', 'WIO', 'NWC'),
                                 feature_group_count=x.shape[-1])
    return y + b.astype(x.dtype), full[:, -(CONV_WIDTH - 1):, :]


def moe(x, router_w, router_b, w_gate_up, b_gate_up, w_down, b_down):
    N, D = x.shape
    E = router_w.shape[-1]
    logits = x.astype(jnp.float32) @ router_w.astype(jnp.float32) + router_b.astype(jnp.float32)
    top_val, top_idx = lax.top_k(logits, TOP_K)
    gate = jax.nn.softmax(top_val, axis=-1)
    nk = N * TOP_K
    flat_e = top_idx.reshape(nk).astype(jnp.int32)
    flat_tok = jnp.arange(nk, dtype=jnp.int32) // TOP_K
    order = jnp.argsort(flat_e, stable=True)
    sorted_e = flat_e[order]
    counts = jnp.bincount(flat_e, length=E).astype(jnp.int32)
    padded = (counts + MOE_BLOCK - 1) // MOE_BLOCK * MOE_BLOCK
    start = jnp.cumsum(counts) - counts
    pend = jnp.cumsum(padded)
    pstart = pend - padded
    dest_sorted = pstart[sorted_e] + jnp.arange(nk, dtype=jnp.int32) - start[sorted_e]
    n_blocks = -(-nk // MOE_BLOCK) + E
    rows = n_blocks * MOE_BLOCK
    slot_tok = jnp.full((rows,), N, jnp.int32).at[dest_sorted].set(flat_tok[order])
    blk_e = jnp.minimum(jnp.searchsorted(pend, jnp.arange(n_blocks, dtype=jnp.int32) * MOE_BLOCK, side='right'), E - 1)
    x_pad = jnp.concatenate([x, jnp.zeros((1, D), x.dtype)], axis=0)
    xb = x_pad[slot_tok].reshape(n_blocks, MOE_BLOCK, D)

    def expert_block(args):
        xblk, e = args
        gu = xblk @ w_gate_up[e] + b_gate_up[e]
        g, u = jnp.split(gu, 2, axis=-1)
        g = jnp.minimum(g, SWIGLU_LIMIT)
        u = jnp.clip(u, -SWIGLU_LIMIT, SWIGLU_LIMIT)
        hdn = (u + 1.0) * (g * jax.nn.sigmoid(SWIGLU_ALPHA * g))
        return hdn @ w_down[e] + b_down[e]

    yb = lax.map(expert_block, (xb, blk_e)).reshape(rows, D)
    dest = jnp.zeros((nk,), jnp.int32).at[order].set(dest_sorted)
    y_assign = yb[dest].reshape(N, TOP_K, D)
    return jnp.einsum('nk,nkd->nd', gate.astype(y_assign.dtype), y_assign)


def encoder_layer(x, pos, ret_state, conv_hist, norm_mix_g, w_in, w_ret_o, conv_dw_w, conv_dw_b, conv_ln_g,
                  conv_ln_b, w_conv_o, w_o, norm_ffn_g, router_w, router_b, w_gate_up, b_gate_up, w_down, b_down):
    B, T, _ = x.shape
    xn = rms_norm(x, norm_mix_g)
    u = xn @ w_in
    s1 = RET_QK_WIDTH
    s2 = s1 + RET_QK_WIDTH
    s3 = s2 + RET_V_WIDTH
    s4 = s3 + RET_V_WIDTH
    s5 = s4 + 2 * CONV_CH
    q, k, v, g_ret, conv_in, gate_logits = jnp.split(u, [s1, s2, s3, s4, s5], axis=-1)
    q = rope(q.reshape(B, T, N_RET_HEADS, RET_QK_DIM), pos)
    k = rope(k.reshape(B, T, N_RET_HEADS, RET_QK_DIM), pos) * (RET_QK_DIM ** -0.5)
    v = v.reshape(B, T, N_RET_HEADS, RET_V_DIM).astype(jnp.float32)
    new_ret, o = retention(ret_state.astype(jnp.float32), q, k, v)
    mu = jnp.mean(o, axis=-1, keepdims=True)
    var = jnp.mean(jnp.square(o - mu), axis=-1, keepdims=True)
    o = ((o - mu) * lax.rsqrt(var + LN_EPS)).reshape(B, T, RET_V_WIDTH).astype(x.dtype)
    branch_ret = (jax.nn.silu(g_ret) * o) @ w_ret_o
    a, b = jnp.split(conv_in, 2, axis=-1)
    glu = a * jax.nn.sigmoid(b)
    c, new_conv = causal_depthwise_conv(conv_hist, glu, conv_dw_w, conv_dw_b)
    branch_conv = jax.nn.silu(layer_norm(c, conv_ln_g, conv_ln_b)) @ w_conv_o
    gate_ret, gate_conv = jnp.split(jax.nn.sigmoid(gate_logits), 2, axis=-1)
    h = x + (gate_ret * branch_ret + gate_conv * branch_conv) @ w_o
    hn = rms_norm(h, norm_ffn_g).reshape(B * T, D_MODEL)
    h = h + moe(hn, router_w, router_b, w_gate_up, b_gate_up, w_down, b_down).reshape(B, T, D_MODEL)
    return h, new_ret.astype(ret_state.dtype), new_conv.astype(conv_hist.dtype)


def setup_inputs(seed: int = 0) -> dict:
    key = jax.random.key(seed)
    ks = jax.random.split(key, 21)

    def nrm(k, shape, s):
        return s * jax.random.normal(k, shape, jnp.float32)

    L = DEPTH
    return {
        'x_prompt': nrm(ks[0], (BATCH, SEQ, D_MODEL), 1.0),
        'x_sample': nrm(ks[1], (DEC_BATCH, DEC_SEQ, D_MODEL), 1.0),
        'state_ret': nrm(ks[2], (L, DEC_BATCH, N_RET_HEADS, RET_QK_DIM, RET_V_DIM), 1.0),
        'cache_conv': nrm(ks[3], (L, DEC_BATCH, CONV_WIDTH - 1, CONV_CH), 0.5),
        'norm_mix_g': 1.0 + nrm(ks[4], (L, D_MODEL), 0.02),
        'w_in': nrm(ks[5], (L, D_MODEL, IN_WIDTH), D_MODEL ** -0.5),
        'w_ret_o': nrm(ks[6], (L, RET_V_WIDTH, D_MODEL), RET_V_WIDTH ** -0.5),
        'conv_dw_w': nrm(ks[7], (L, CONV_WIDTH, CONV_CH), CONV_WIDTH ** -0.5),
        'conv_dw_b': nrm(ks[8], (L, CONV_CH), 0.01),
        'conv_ln_g': 1.0 + nrm(ks[9], (L, CONV_CH), 0.02),
        'conv_ln_b': nrm(ks[10], (L, CONV_CH), 0.01),
        'w_conv_o': nrm(ks[11], (L, CONV_CH, D_MODEL), CONV_CH ** -0.5),
        'w_o': nrm(ks[12], (L, D_MODEL, D_MODEL), D_MODEL ** -0.5),
        'norm_ffn_g': 1.0 + nrm(ks[13], (L, D_MODEL), 0.02),
        'router_w': nrm(ks[14], (L, D_MODEL, N_EXPERTS), D_MODEL ** -0.5),
        'router_b': nrm(ks[15], (L, N_EXPERTS), 0.01),
        'w_gate_up': nrm(ks[16], (L, N_EXPERTS, D_MODEL, 2 * D_EXPERT), D_MODEL ** -0.5),
        'b_gate_up': nrm(ks[17], (L, N_EXPERTS, 2 * D_EXPERT), 0.01),
        'w_down': nrm(ks[18], (L, N_EXPERTS, D_EXPERT, D_MODEL), D_EXPERT ** -0.5),
        'b_down': nrm(ks[19], (L, N_EXPERTS, D_MODEL), 0.01),
        'norm_final_g': 1.0 + nrm(ks[20], (D_MODEL,), 0.02),
    }


def reference(x_prompt, x_sample, state_ret, cache_conv, norm_mix_g, w_in, w_ret_o, conv_dw_w, conv_dw_b,
              conv_ln_g, conv_ln_b, w_conv_o, w_o, norm_ffn_g, router_w, router_b, w_gate_up, b_gate_up,
              w_down, b_down, norm_final_g):
    B, T, _ = x_prompt.shape
    Ts = x_sample.shape[1]
    pos_p = jnp.arange(T, dtype=jnp.int32)
    pos_s = PAST_LEN + jnp.arange(Ts, dtype=jnp.int32)
    hp, hs = x_prompt, x_sample
    ret_p, conv_p, ret_s, conv_s = [], [], [], []
    for l in range(DEPTH):
        w = (norm_mix_g[l], w_in[l], w_ret_o[l], conv_dw_w[l], conv_dw_b[l], conv_ln_g[l], conv_ln_b[l],
             w_conv_o[l], w_o[l], norm_ffn_g[l], router_w[l], router_b[l], w_gate_up[l], b_gate_up[l],
             w_down[l], b_down[l])
        zero_ret = jnp.zeros((B,) + state_ret.shape[2:], state_ret.dtype)
        zero_conv = jnp.zeros((B,) + cache_conv.shape[2:], cache_conv.dtype)
        hp, r, c = encoder_layer(hp, pos_p, zero_ret, zero_conv, *w)
        ret_p.append(r)
        conv_p.append(c)
        hs, r, c = encoder_layer(hs, pos_s, state_ret[l], cache_conv[l], *w)
        ret_s.append(r)
        conv_s.append(c)
    y_prompt = rms_norm(hp, norm_final_g)
    y_sample = rms_norm(hs, norm_final_g)
    return (y_prompt, y_sample, jnp.stack(ret_p), jnp.stack(conv_p), jnp.stack(ret_s), jnp.stack(conv_s))
```

```python
import functools
import math

import jax
import jax.numpy as jnp
from jax import lax
from jax.experimental import pallas as pl
from jax.experimental.pallas import tpu as pltpu

TOP_K = 4
PAST_LEN = 1024
ROPE_BASE = 10000.0
RMS_EPS = 1e-6
LN_EPS = 1e-5
SWIGLU_LIMIT = 7.0
SWIGLU_ALPHA = 1.702

F32 = jnp.float32
BF16 = jnp.bfloat16
U32 = jnp.uint32

V7X_VMEM_BYTES = 64 * 1024 * 1024
LANES = 128

NORM_ROWS = 512
PROJ_TM = 1024
PROJ_TN = 1024
PROJ_SUB = 256
RET_CHUNK_MAX = 256
CONV_TT_MAX = 256
CONV_RB = 128
MERGE_TM = 256
MOE_TM = 512
MOE1_TF = 512
MOE2_TN = 1024
MOE_SUB = 256
DISPATCH_TOK = 256
COMBINE_TOK = 128


def _cparams(sem, vmem_mb):
    return pltpu.CompilerParams(dimension_semantics=sem, vmem_limit_bytes=vmem_mb * 1024 * 1024)


def _rmsnorm_kernel(x_ref, g_ref, o_ref):
    x = x_ref[...]
    ms = jnp.mean(x * x, axis=-1, keepdims=True)
    o_ref[...] = (x * lax.rsqrt(ms + RMS_EPS) * g_ref[...]).astype(o_ref.dtype)


def _rmsnorm_bf16(x, g):
    n, d = x.shape
    return pl.pallas_call(
        _rmsnorm_kernel,
        grid=(n // NORM_ROWS,),
        in_specs=[pl.BlockSpec((NORM_ROWS, d), lambda i: (i, 0)),
                  pl.BlockSpec((1, d), lambda i: (0, 0))],
        out_specs=pl.BlockSpec((NORM_ROWS, d), lambda i: (i, 0)),
        out_shape=jax.ShapeDtypeStruct((n, d), BF16),
        compiler_params=_cparams(("arbitrary",), 32),
        name="rmsnorm_bf16",
    )(x, g.reshape(1, d))


def _inproj_kernel(x_ref, w_ref, o_ref, wbf_ref):
    @pl.when(pl.program_id(1) == 0)
    def _():
        wbf_ref[...] = w_ref[...].astype(BF16)

    def body(r, carry):
        rows = pl.ds(pl.multiple_of(r * PROJ_SUB, PROJ_SUB), PROJ_SUB)
        o_ref[rows, :] = jnp.dot(x_ref[rows, :], wbf_ref[...], preferred_element_type=F32)
        return carry

    lax.fori_loop(0, PROJ_TM // PROJ_SUB, body, 0)


def _input_projection(xn, w):
    n, d = xn.shape
    width = w.shape[1]
    return pl.pallas_call(
        _inproj_kernel,
        grid=(width // PROJ_TN, n // PROJ_TM),
        in_specs=[pl.BlockSpec((PROJ_TM, d), lambda j, i: (i, 0)),
                  pl.BlockSpec((d, PROJ_TN), lambda j, i: (0, j))],
        out_specs=pl.BlockSpec((PROJ_TM, PROJ_TN), lambda j, i: (i, j)),
        out_shape=jax.ShapeDtypeStruct((n, width), F32),
        scratch_shapes=[pltpu.VMEM((d, PROJ_TN), BF16)],
        compiler_params=_cparams(("arbitrary", "arbitrary"), 48),
        name="input_projection",
    )(xn, w)


def _retention_kernel(q_ref, k_ref, v_ref, g_ref, cos_ref, sin_ref, s0_ref, r_ref, sout_ref, state_ref,
                      *, chunk, heads, dk, dv):
    c = pl.program_id(1)

    @pl.when(c == 0)
    def _():
        state_ref[...] = s0_ref[0]

    cos = cos_ref[...]
    sin = sin_ref[...]
    li = lax.broadcasted_iota(jnp.int32, (chunk, chunk), 0)
    mi = lax.broadcasted_iota(jnp.int32, (chunk, chunk), 1)
    rel = (li - mi).astype(F32)
    row = lax.broadcasted_iota(jnp.int32, (chunk, 1), 0).astype(F32)
    for h in range(heads):
        lg = math.log1p(-(2.0 ** (-5.0 - h)))
        q = q_ref[:, h * dk:(h + 1) * dk]
        k = k_ref[:, h * dk:(h + 1) * dk]
        v = v_ref[:, h * dv:(h + 1) * dv].astype(BF16)
        qr = (q * cos + pltpu.roll(q, dk // 2, 1) * sin).astype(BF16)
        kr = (k * cos + pltpu.roll(k, dk // 2, 1) * sin) * (dk ** -0.5)
        decay = jnp.where(rel >= 0, jnp.exp(lg * jnp.maximum(rel, 0.0)), 0.0)
        scores = lax.dot_general(qr, kr.astype(BF16), (((1,), (1,)), ((), ())),
                                 preferred_element_type=F32) * decay
        intra = jnp.dot(scores.astype(BF16), v, preferred_element_type=F32)
        st = state_ref[h]
        cross = jnp.dot(qr, st.astype(BF16), preferred_element_type=F32) * jnp.exp(lg * (row + 1.0))
        o = intra + cross
        kdec = (kr * jnp.exp(lg * (chunk - 1.0 - row))).astype(BF16)
        state_ref[h] = math.exp(lg * chunk) * st + lax.dot_general(
            kdec, v, (((0,), (0,)), ((), ())), preferred_element_type=F32)
        mu = jnp.mean(o, axis=-1, keepdims=True)
        var = jnp.mean(jnp.square(o - mu), axis=-1, keepdims=True)
        on = (o - mu) * lax.rsqrt(var + LN_EPS)
        g = g_ref[:, h * dv:(h + 1) * dv]
        r_ref[:, h * dv:(h + 1) * dv] = (g * jax.nn.sigmoid(g) * on).astype(r_ref.dtype)

    @pl.when(c == pl.num_programs(1) - 1)
    def _():
        sout_ref[0] = state_ref[...]


def _retention(u, row0, batch, seq, state0, cos, sin):
    _, heads, dk, dv = state0.shape
    chunk = min(seq, RET_CHUNK_MAX)
    nc = seq // chunk
    rb0 = row0 // chunk
    qw, vw = heads * dk, heads * dv

    def rows(b, c):
        return rb0 + b * nc + c

    kern = functools.partial(_retention_kernel, chunk=chunk, heads=heads, dk=dk, dv=dv)
    return pl.pallas_call(
        kern,
        grid=(batch, nc),
        in_specs=[pl.BlockSpec((chunk, qw), lambda b, c: (rows(b, c), 0)),
                  pl.BlockSpec((chunk, qw), lambda b, c: (rows(b, c), 1)),
                  pl.BlockSpec((chunk, vw), lambda b, c: (rows(b, c), (2 * qw) // vw)),
                  pl.BlockSpec((chunk, vw), lambda b, c: (rows(b, c), (2 * qw) // vw + 1)),
                  pl.BlockSpec((chunk, dk), lambda b, c: (c, 0)),
                  pl.BlockSpec((chunk, dk), lambda b, c: (c, 0)),
                  pl.BlockSpec((1, heads, dk, dv), lambda b, c: (b, 0, 0, 0))],
        out_specs=[pl.BlockSpec((chunk, vw), lambda b, c: (b * nc + c, 0)),
                   pl.BlockSpec((1, heads, dk, dv), lambda b, c: (b, 0, 0, 0))],
        out_shape=[jax.ShapeDtypeStruct((batch * seq, vw), BF16),
                   jax.ShapeDtypeStruct(state0.shape, F32)],
        scratch_shapes=[pltpu.VMEM((heads, dk, dv), F32)],
        compiler_params=_cparams(("arbitrary", "arbitrary"), 48),
        name="retention",
    )(u, u, u, u, cos, sin, state0)


def _conv_kernel(a_ref, b_ref, hist_ref, w_ref, bias_ref, lng_ref, lnb_ref, o_ref, cache_ref, s_ref, y_ref,
                 *, tt, width, off):
    t = pl.program_id(1)
    hl = width - 1
    ch = a_ref.shape[1]

    @pl.when(t == 0)
    def _():
        s_ref[off - hl:off, :] = hist_ref[0]

    @pl.when(t > 0)
    def _():
        s_ref[off - hl:off, :] = s_ref[off + tt - hl:off + tt, :]

    s_ref[off:off + tt, :] = a_ref[...] * jax.nn.sigmoid(b_ref[...])
    cache_ref[0] = s_ref[off + tt - hl:off + tt, :]

    rb = min(CONV_RB, tt)
    for cc in range(ch // LANES):
        cols = slice(cc * LANES, (cc + 1) * LANES)
        for r in range(tt // rb):
            acc = jnp.broadcast_to(bias_ref[:, cols], (rb, LANES))
            for j in range(width):
                st = off - hl + j + r * rb
                acc = acc + w_ref[j:j + 1, cols] * s_ref[st:st + rb, cols]
            y_ref[r * rb:(r + 1) * rb, cols] = acc

    y = y_ref[...]
    mu = jnp.mean(y, axis=-1, keepdims=True)
    var = jnp.mean(jnp.square(y - mu), axis=-1, keepdims=True)
    yn = (y - mu) * lax.rsqrt(var + LN_EPS) * lng_ref[...] + lnb_ref[...]
    o_ref[...] = (yn * jax.nn.sigmoid(yn)).astype(o_ref.dtype)


def _conv_module(u, row0, batch, seq, hist, conv_w, conv_b, ln_g, ln_b, a_col):
    width, ch = conv_w.shape
    tt = min(seq, CONV_TT_MAX)
    nt = seq // tt
    rb0 = row0 // tt
    off = 32
    assert width - 1 <= off and tt >= width - 1
    kern = functools.partial(_conv_kernel, tt=tt, width=width, off=off)
    vec = pl.BlockSpec((1, ch), lambda b, t: (0, 0))
    return pl.pallas_call(
        kern,
        grid=(batch, nt),
        in_specs=[pl.BlockSpec((tt, ch), lambda b, t: (rb0 + b * nt + t, a_col)),
                  pl.BlockSpec((tt, ch), lambda b, t: (rb0 + b * nt + t, a_col + 1)),
                  pl.BlockSpec((1, width - 1, ch), lambda b, t: (b, 0, 0)),
                  pl.BlockSpec((width, ch), lambda b, t: (0, 0)),
                  vec, vec, vec],
        out_specs=[pl.BlockSpec((tt, ch), lambda b, t: (b * nt + t, 0)),
                   pl.BlockSpec((1, width - 1, ch), lambda b, t: (b, 0, 0))],
        out_shape=[jax.ShapeDtypeStruct((batch * seq, ch), BF16),
                   jax.ShapeDtypeStruct((batch, width - 1, ch), F32)],
        scratch_shapes=[pltpu.VMEM((off + tt, ch), F32), pltpu.VMEM((tt, ch), F32)],
        compiler_params=_cparams(("arbitrary", "arbitrary"), 32),
        name="conv_module",
    )(u, u, hist, conv_w, conv_b.reshape(1, ch), ln_g.reshape(1, ch), ln_b.reshape(1, ch))


def _merge_kernel(r_ref, cv_ref, glr_ref, glc_ref, x_ref, wr_ref, wc_ref, wo_ref, gf_ref, rw_ref, rb_ref,
                  h_ref, hnp_ref, lg_ref):
    br = jnp.dot(r_ref[...], wr_ref[...], preferred_element_type=F32)
    bc = jnp.dot(cv_ref[...], wc_ref[...], preferred_element_type=F32)
    m = jax.nn.sigmoid(glr_ref[...]) * br + jax.nn.sigmoid(glc_ref[...]) * bc
    h = x_ref[...] + jnp.dot(m.astype(BF16), wo_ref[...], preferred_element_type=F32)
    h_ref[...] = h
    hn = h * lax.rsqrt(jnp.mean(h * h, axis=-1, keepdims=True) + RMS_EPS) * gf_ref[...]
    half = hn.shape[1] // 2
    a = hn[:, :half].astype(BF16)
    b = hn[:, half:].astype(BF16)
    lg_ref[...] = (jnp.dot(a, rw_ref[:half, :], preferred_element_type=F32)
                   + jnp.dot(b, rw_ref[half:, :], preferred_element_type=F32) + rb_ref[...])
    au = lax.bitcast_convert_type(a.astype(F32), U32)
    bu = lax.bitcast_convert_type(b.astype(F32), U32)
    hnp_ref[...] = au | (bu >> 16)


def _merge(r, cv, u, x, w_ret_o, w_conv_o, w_o, g_ffn, router_w, router_b, gate_col):
    n, d = x.shape
    ch = cv.shape[1]
    e_pad = router_w.shape[1]
    tm = MERGE_TM

    def const(shape):
        return pl.BlockSpec(shape, lambda i: (0,) * len(shape), pipeline_mode=pl.Buffered(1))

    return pl.pallas_call(
        _merge_kernel,
        grid=(n // tm,),
        in_specs=[pl.BlockSpec((tm, r.shape[1]), lambda i: (i, 0)),
                  pl.BlockSpec((tm, ch), lambda i: (i, 0)),
                  pl.BlockSpec((tm, d), lambda i: (i, gate_col)),
                  pl.BlockSpec((tm, d), lambda i: (i, gate_col + 1)),
                  pl.BlockSpec((tm, d), lambda i: (i, 0)),
                  const(w_ret_o.shape), const(w_conv_o.shape), const(w_o.shape),
                  const((1, d)), const(router_w.shape), const((1, e_pad))],
        out_specs=[pl.BlockSpec((tm, d), lambda i: (i, 0)),
                   pl.BlockSpec((tm, d // 2), lambda i: (i, 0)),
                   pl.BlockSpec((tm, e_pad), lambda i: (i, 0))],
        out_shape=[jax.ShapeDtypeStruct((n, d), F32),
                   jax.ShapeDtypeStruct((n, d // 2), U32),
                   jax.ShapeDtypeStruct((n, e_pad), F32)],
        compiler_params=_cparams(("arbitrary",), 56),
        name="merge_router",
    )(r, cv, u, u, x, w_ret_o, w_conv_o, w_o, g_ffn.reshape(1, d), router_w, router_b)


def _dispatch_kernel(dest_ref, src_ref, init_ref, out_ref, sem):
    del init_ref
    ntok = dest_ref.shape[2] // TOP_K
    base = pl.program_id(0) * ntok

    def copy(t, k):
        return pltpu.make_async_copy(src_ref.at[pl.ds(base + t, 1)],
                                     out_ref.at[pl.ds(dest_ref[0, 0, t * TOP_K + k], 1)], sem)

    def start(t, carry):
        for k in range(TOP_K):
            copy(t, k).start()
        return carry

    def wait(t, carry):
        for k in range(TOP_K):
            copy(t, k).wait()
        return carry

    lax.fori_loop(0, ntok, start, 0)
    lax.fori_loop(0, ntok, wait, 0)


def _dispatch(hnp, dest, rows_pad):
    n, dw = hnp.shape
    nsteps = n // DISPATCH_TOK
    init = jnp.zeros((rows_pad, dw), U32)
    return pl.pallas_call(
        _dispatch_kernel,
        grid=(nsteps,),
        in_specs=[pl.BlockSpec((1, 1, DISPATCH_TOK * TOP_K), lambda i: (i, 0, 0), memory_space=pltpu.SMEM),
                  pl.BlockSpec(memory_space=pl.ANY),
                  pl.BlockSpec(memory_space=pl.ANY)],
        out_specs=pl.BlockSpec(memory_space=pl.ANY),
        out_shape=jax.ShapeDtypeStruct((rows_pad, dw), U32),
        scratch_shapes=[pltpu.SemaphoreType.DMA(())],
        input_output_aliases={2: 0},
        compiler_params=_cparams(("arbitrary",), 32),
        name="moe_dispatch",
    )(dest.reshape(nsteps, 1, DISPATCH_TOK * TOP_K), hnp, init)


def _unpack_bf16_pair(w):
    hi = lax.bitcast_convert_type(w & jnp.uint32(0xFFFF0000), F32).astype(BF16)
    lo = lax.bitcast_convert_type(w << 16, F32).astype(BF16)
    return hi, lo


def _moe1_kernel(be_ref, nb_ref, x_ref, wg_ref, wu_ref, bg_ref, bu_ref, o_ref, wbf_ref):
    b = pl.program_id(1)
    tf = wg_ref.shape[2]
    valid = b < nb_ref[0]
    first = jnp.logical_or(b == 0, be_ref[b] != be_ref[jnp.maximum(b - 1, 0)])

    @pl.when(jnp.logical_and(valid, first))
    def _():
        wbf_ref[:, :tf] = wg_ref[0].astype(BF16)
        wbf_ref[:, tf:] = wu_ref[0].astype(BF16)

    @pl.when(valid)
    def _():
        half = x_ref.shape[1]

        def body(r, carry):
            rows = pl.ds(pl.multiple_of(r * MOE_SUB, MOE_SUB), MOE_SUB)
            hi, lo = _unpack_bf16_pair(x_ref[rows, :])
            gu = (jnp.dot(hi, wbf_ref[:half, :], preferred_element_type=F32)
                  + jnp.dot(lo, wbf_ref[half:, :], preferred_element_type=F32))
            g = jnp.minimum(gu[:, :tf] + bg_ref[0], SWIGLU_LIMIT)
            u = jnp.clip(gu[:, tf:] + bu_ref[0], -SWIGLU_LIMIT, SWIGLU_LIMIT)
            o_ref[rows, :] = ((u + 1.0) * (g * jax.nn.sigmoid(SWIGLU_ALPHA * g))).astype(o_ref.dtype)
            return carry

        lax.fori_loop(0, x_ref.shape[0] // MOE_SUB, body, 0)

    @pl.when(jnp.logical_not(valid))
    def _():
        o_ref[...] = jnp.zeros_like(o_ref)


def _moe_gate_up(xs, blk_e, nblk, w_gate_up, b_gate_up):
    rows_pad, dw = xs.shape
    e, d, f2 = w_gate_up.shape
    f = f2 // 2
    tf = MOE1_TF
    nb = rows_pad // MOE_TM
    nj = f // tf

    def xrow(j, b, be, nbr):
        return (jnp.minimum(b, nbr[0] - 1), 0)

    grid_spec = pltpu.PrefetchScalarGridSpec(
        num_scalar_prefetch=2,
        grid=(nj, nb),
        in_specs=[pl.BlockSpec((MOE_TM, dw), xrow),
                  pl.BlockSpec((1, d, tf), lambda j, b, be, nbr: (be[b], 0, j)),
                  pl.BlockSpec((1, d, tf), lambda j, b, be, nbr: (be[b], 0, nj + j)),
                  pl.BlockSpec((1, 1, tf), lambda j, b, be, nbr: (be[b], 0, j)),
                  pl.BlockSpec((1, 1, tf), lambda j, b, be, nbr: (be[b], 0, nj + j))],
        out_specs=pl.BlockSpec((MOE_TM, tf), lambda j, b, be, nbr: (b, j)),
        scratch_shapes=[pltpu.VMEM((d, 2 * tf), BF16)],
    )
    return pl.pallas_call(
        _moe1_kernel,
        grid_spec=grid_spec,
        out_shape=jax.ShapeDtypeStruct((rows_pad, f), BF16),
        compiler_params=_cparams(("arbitrary", "arbitrary"), 48),
        name="moe_gate_up",
    )(blk_e, nblk, xs, w_gate_up, w_gate_up, b_gate_up.reshape(e, 1, f2), b_gate_up.reshape(e, 1, f2))


def _moe2_kernel(be_ref, nb_ref, x_ref, w_ref, bias_ref, o_ref, wbf_ref):
    b = pl.program_id(1)
    valid = b < nb_ref[0]
    first = jnp.logical_or(b == 0, be_ref[b] != be_ref[jnp.maximum(b - 1, 0)])

    @pl.when(jnp.logical_and(valid, first))
    def _():
        wbf_ref[...] = w_ref[0].astype(BF16)

    @pl.when(valid)
    def _():
        def body(r, carry):
            rows = pl.ds(pl.multiple_of(r * MOE_SUB, MOE_SUB), MOE_SUB)
            o_ref[rows, :] = jnp.dot(x_ref[rows, :], wbf_ref[...], preferred_element_type=F32) + bias_ref[0]
            return carry

        lax.fori_loop(0, x_ref.shape[0] // MOE_SUB, body, 0)

    @pl.when(jnp.logical_not(valid))
    def _():
        o_ref[...] = jnp.zeros_like(o_ref)


def _moe_down(hdn, blk_e, nblk, w_down, b_down):
    rows_pad, f = hdn.shape
    e, _, d = w_down.shape
    tn = MOE2_TN
    nb = rows_pad // MOE_TM
    grid_spec = pltpu.PrefetchScalarGridSpec(
        num_scalar_prefetch=2,
        grid=(d // tn, nb),
        in_specs=[pl.BlockSpec((MOE_TM, f), lambda j, b, be, nbr: (jnp.minimum(b, nbr[0] - 1), 0)),
                  pl.BlockSpec((1, f, tn), lambda j, b, be, nbr: (be[b], 0, j)),
                  pl.BlockSpec((1, 1, tn), lambda j, b, be, nbr: (be[b], 0, j))],
        out_specs=pl.BlockSpec((MOE_TM, tn), lambda j, b, be, nbr: (b, j)),
        scratch_shapes=[pltpu.VMEM((f, tn), BF16)],
    )
    return pl.pallas_call(
        _moe2_kernel,
        grid_spec=grid_spec,
        out_shape=jax.ShapeDtypeStruct((rows_pad, d), F32),
        compiler_params=_cparams(("arbitrary", "arbitrary"), 48),
        name="moe_down",
    )(blk_e, nblk, hdn, w_down, b_down.reshape(e, 1, d))


def _combine_kernel(dest_ref, y_ref, gate_ref, h_ref, g_ref, o_ref, buf_ref, sem):
    ntok = h_ref.shape[0]

    def copy(t, k):
        return pltpu.make_async_copy(y_ref.at[pl.ds(dest_ref[0, 0, t * TOP_K + k], 1)],
                                     buf_ref.at[k, pl.ds(t, 1)], sem)

    def start(t, carry):
        for k in range(TOP_K):
            copy(t, k).start()
        return carry

    def wait(t, carry):
        for k in range(TOP_K):
            copy(t, k).wait()
        return carry

    lax.fori_loop(0, ntok, start, 0)
    lax.fori_loop(0, ntok, wait, 0)
    gate = gate_ref[...]
    acc = h_ref[...]
    for k in range(TOP_K):
        acc = acc + gate[:, k:k + 1] * buf_ref[k]
    o_ref[...] = acc * lax.rsqrt(jnp.mean(acc * acc, axis=-1, keepdims=True) + RMS_EPS) * g_ref[...]


def _combine(yb, dest, gate, h, g_final):
    n, d = h.shape
    tok = COMBINE_TOK
    nsteps = n // tok
    return pl.pallas_call(
        _combine_kernel,
        grid=(nsteps,),
        in_specs=[pl.BlockSpec((1, 1, tok * TOP_K), lambda i: (i, 0, 0), memory_space=pltpu.SMEM),
                  pl.BlockSpec(memory_space=pl.ANY),
                  pl.BlockSpec((tok, TOP_K), lambda i: (i, 0)),
                  pl.BlockSpec((tok, d), lambda i: (i, 0)),
                  pl.BlockSpec((1, d), lambda i: (0, 0))],
        out_specs=pl.BlockSpec((tok, d), lambda i: (i, 0)),
        out_shape=jax.ShapeDtypeStruct((n, d), F32),
        scratch_shapes=[pltpu.VMEM((TOP_K, tok, d), F32), pltpu.SemaphoreType.DMA(())],
        compiler_params=_cparams(("arbitrary",), 32),
        name="moe_combine",
    )(dest.reshape(nsteps, 1, tok * TOP_K), yb, gate, h, g_final.reshape(1, d))


def _routing(logits, n_experts, rows_pad):
    top_val, top_idx = lax.top_k(logits, TOP_K)
    gate = jax.nn.softmax(top_val, axis=-1)
    flat_e = top_idx.reshape(-1).astype(jnp.int32)
    onehot = (flat_e[:, None] == jnp.arange(n_experts, dtype=jnp.int32)[None, :]).astype(jnp.int32)
    before = jnp.cumsum(onehot, axis=0) - onehot
    rank = jnp.sum(before * onehot, axis=1)
    counts = jnp.sum(onehot, axis=0)
    nblk_e = (counts + MOE_TM - 1) // MOE_TM
    blk_end = jnp.cumsum(nblk_e)
    pstart = (blk_end - nblk_e) * MOE_TM
    dest = jnp.sum(onehot * pstart[None, :], axis=1) + rank
    nblk = blk_end[-1]
    nb = rows_pad // MOE_TM
    bidx = jnp.minimum(jnp.arange(nb, dtype=jnp.int32), nblk - 1)
    blk_e = jnp.sum((bidx[:, None] >= blk_end[None, :]).astype(jnp.int32), axis=1)
    return gate, dest.astype(jnp.int32), blk_e.astype(jnp.int32), nblk.reshape(1).astype(jnp.int32)


def _rope_tables(pos, dk):
    half = dk // 2
    freq = ROPE_BASE ** (-jnp.arange(half, dtype=F32) / half)
    ang = pos.astype(F32)[:, None] * freq[None, :]
    cos, sin = jnp.cos(ang), jnp.sin(ang)
    return jnp.concatenate([cos, cos], axis=-1), jnp.concatenate([-sin, sin], axis=-1)


def kernel(x_prompt, x_sample, state_ret, cache_conv, norm_mix_g, w_in, w_ret_o, conv_dw_w, conv_dw_b, conv_ln_g, conv_ln_b, w_conv_o, w_o, norm_ffn_g, router_w, router_b, w_gate_up, b_gate_up, w_down, b_down, norm_final_g):
    bp, tp, d = x_prompt.shape
    bs, ts, _ = x_sample.shape
    depth, _, heads, dk, dv = state_ret.shape
    ch = cache_conv.shape[-1]
    n_experts = router_w.shape[-1]
    np_rows, ns_rows = bp * tp, bs * ts
    n = np_rows + ns_rows
    qw, vw = heads * dk, heads * dv
    a_col = (2 * qw + 2 * vw) // ch
    gate_col = (2 * qw + 2 * vw + 2 * ch) // d
    rows_pad = (n * TOP_K // MOE_TM + n_experts) * MOE_TM

    cos_p, sin_p = _rope_tables(jnp.arange(tp, dtype=jnp.int32), dk)
    cos_s, sin_s = _rope_tables(PAST_LEN + jnp.arange(ts, dtype=jnp.int32), dk)

    h = jnp.concatenate([x_prompt.reshape(np_rows, d), x_sample.reshape(ns_rows, d)], axis=0)
    ret_p, conv_p, ret_s, conv_s = [], [], [], []
    for l in range(depth):
        xn = _rmsnorm_bf16(h, norm_mix_g[l])
        u = _input_projection(xn, w_in[l])

        zero_ret = jnp.zeros((bp,) + state_ret.shape[2:], F32)
        zero_conv = jnp.zeros((bp,) + cache_conv.shape[2:], F32)
        r_p, s_p = _retention(u, 0, bp, tp, zero_ret, cos_p, sin_p)
        r_s, s_s = _retention(u, np_rows, bs, ts, state_ret[l], cos_s, sin_s)
        c_p, cc_p = _conv_module(u, 0, bp, tp, zero_conv, conv_dw_w[l], conv_dw_b[l], conv_ln_g[l],
                                 conv_ln_b[l], a_col)
        c_s, cc_s = _conv_module(u, np_rows, bs, ts, cache_conv[l], conv_dw_w[l], conv_dw_b[l], conv_ln_g[l],
                                 conv_ln_b[l], a_col)
        ret_p.append(s_p)
        ret_s.append(s_s)
        conv_p.append(cc_p)
        conv_s.append(cc_s)
        r = jnp.concatenate([r_p, r_s], axis=0)
        cv = jnp.concatenate([c_p, c_s], axis=0)

        e_pad = -(-n_experts // LANES) * LANES
        rw = jnp.pad(router_w[l], ((0, 0), (0, e_pad - n_experts))).astype(BF16)
        rb = jnp.pad(router_b[l], (0, e_pad - n_experts)).reshape(1, e_pad)
        hmid, hnp, logits = _merge(r, cv, u, h, w_ret_o[l].astype(BF16), w_conv_o[l].astype(BF16),
                                   w_o[l].astype(BF16), norm_ffn_g[l], rw, rb, gate_col)

        gate, dest, blk_e, nblk = _routing(logits[:, :n_experts], n_experts, rows_pad)
        xs = _dispatch(hnp, dest, rows_pad)
        hdn = _moe_gate_up(xs, blk_e, nblk, w_gate_up[l], b_gate_up[l])
        yb = _moe_down(hdn, blk_e, nblk, w_down[l], b_down[l])
        if l == depth - 1:
            y = _combine(yb, dest, gate, hmid, norm_final_g)
        else:
            raise NotImplementedError("depth > 1 needs a combine without the final norm")
        h = y
    y_prompt = h[:np_rows].reshape(bp, tp, d)
    y_sample = h[np_rows:].reshape(bs, ts, d)
    return (y_prompt, y_sample, jnp.stack(ret_p), jnp.stack(conv_p), jnp.stack(ret_s), jnp.stack(conv_s))
```

```python
import functools
import math

import jax
import jax.numpy as jnp
from jax import lax
from jax.experimental import pallas as pl
from jax.experimental.pallas import tpu as pltpu

TOP_K = 4
PAST_LEN = 1024
ROPE_BASE = 10000.0
RMS_EPS = 1e-6
LN_EPS = 1e-5
SWIGLU_LIMIT = 7.0
SWIGLU_ALPHA = 1.702

F32 = jnp.float32
BF16 = jnp.bfloat16
U32 = jnp.uint32

V7X_VMEM_BYTES = 64 * 1024 * 1024
LANES = 128

NORM_ROWS = 512
PROJ_TM = 1024
PROJ_TN = 1024
PROJ_SUB = 256
RET_CHUNK_MAX = 256
CONV_TT_MAX = 256
CONV_RB = 128
MERGE_TM = 256
MOE_TM = 512
MOE1_TF = 512
MOE2_TN = 1024
MOE_SUB = 256
DISPATCH_TOK = 256
COMBINE_TOK = 128


def _cparams(sem, vmem_mb):
    return pltpu.CompilerParams(dimension_semantics=sem, vmem_limit_bytes=vmem_mb * 1024 * 1024)


def _rmsnorm_kernel(xa_ref, xb_ref, g_ref, o_ref, *, a_steps):
    def norm(x_ref):
        x = x_ref[...]
        ms = jnp.mean(x * x, axis=-1, keepdims=True)
        o_ref[...] = (x * lax.rsqrt(ms + RMS_EPS) * g_ref[...]).astype(o_ref.dtype)

    @pl.when(pl.program_id(0) < a_steps)
    def _():
        norm(xa_ref)

    @pl.when(pl.program_id(0) >= a_steps)
    def _():
        norm(xb_ref)


def _rmsnorm_bf16(xa, xb, g):
    (na, d), nb = xa.shape, xb.shape[0]
    a_steps = na // NORM_ROWS
    return pl.pallas_call(
        functools.partial(_rmsnorm_kernel, a_steps=a_steps),
        grid=((na + nb) // NORM_ROWS,),
        in_specs=[pl.BlockSpec((NORM_ROWS, d), lambda i: (jnp.minimum(i, a_steps - 1), 0)),
                  pl.BlockSpec((NORM_ROWS, d), lambda i: (jnp.maximum(i - a_steps, 0), 0)),
                  pl.BlockSpec((1, d), lambda i: (0, 0))],
        out_specs=pl.BlockSpec((NORM_ROWS, d), lambda i: (i, 0)),
        out_shape=jax.ShapeDtypeStruct((na + nb, d), BF16),
        compiler_params=_cparams(("arbitrary",), 32),
        name="rmsnorm_bf16",
    )(xa, xb, g.reshape(1, d))


def _inproj_kernel(x_ref, w_ref, o_ref, wbf_ref):
    @pl.when(pl.program_id(1) == 0)
    def _():
        wbf_ref[...] = w_ref[...].astype(BF16)

    def body(r, carry):
        rows = pl.ds(pl.multiple_of(r * PROJ_SUB, PROJ_SUB), PROJ_SUB)
        o_ref[rows, :] = jnp.dot(x_ref[rows, :], wbf_ref[...], preferred_element_type=F32)
        return carry

    lax.fori_loop(0, PROJ_TM // PROJ_SUB, body, 0)


def _input_projection(xn, w):
    n, d = xn.shape
    width = w.shape[1]
    return pl.pallas_call(
        _inproj_kernel,
        grid=(width // PROJ_TN, n // PROJ_TM),
        in_specs=[pl.BlockSpec((PROJ_TM, d), lambda j, i: (i, 0)),
                  pl.BlockSpec((d, PROJ_TN), lambda j, i: (0, j))],
        out_specs=pl.BlockSpec((PROJ_TM, PROJ_TN), lambda j, i: (i, j)),
        out_shape=jax.ShapeDtypeStruct((n, width), F32),
        scratch_shapes=[pltpu.VMEM((d, PROJ_TN), BF16)],
        compiler_params=_cparams(("arbitrary", "arbitrary"), 48),
        name="input_projection",
    )(xn, w)


def _retention_kernel(q_ref, k_ref, v_ref, g_ref, cos_ref, sin_ref, s0_ref, r_ref, sout_ref, state_ref,
                      *, chunk, heads, dk, dv):
    c = pl.program_id(1)

    @pl.when(c == 0)
    def _():
        state_ref[...] = s0_ref[0]

    cos = cos_ref[...]
    sin = sin_ref[...]
    li = lax.broadcasted_iota(jnp.int32, (chunk, chunk), 0)
    mi = lax.broadcasted_iota(jnp.int32, (chunk, chunk), 1)
    rel = (li - mi).astype(F32)
    row = lax.broadcasted_iota(jnp.int32, (chunk, 1), 0).astype(F32)
    for h in range(heads):
        lg = math.log1p(-(2.0 ** (-5.0 - h)))
        q = q_ref[:, h * dk:(h + 1) * dk]
        k = k_ref[:, h * dk:(h + 1) * dk]
        v = v_ref[:, h * dv:(h + 1) * dv].astype(BF16)
        qr = (q * cos + pltpu.roll(q, dk // 2, 1) * sin).astype(BF16)
        kr = (k * cos + pltpu.roll(k, dk // 2, 1) * sin) * (dk ** -0.5)
        decay = jnp.where(rel >= 0, jnp.exp(lg * jnp.maximum(rel, 0.0)), 0.0)
        scores = lax.dot_general(qr, kr.astype(BF16), (((1,), (1,)), ((), ())),
                                 preferred_element_type=F32) * decay
        intra = jnp.dot(scores.astype(BF16), v, preferred_element_type=F32)
        st = state_ref[h]
        cross = jnp.dot(qr, st.astype(BF16), preferred_element_type=F32) * jnp.exp(lg * (row + 1.0))
        o = intra + cross
        kdec = (kr * jnp.exp(lg * (chunk - 1.0 - row))).astype(BF16)
        state_ref[h] = math.exp(lg * chunk) * st + lax.dot_general(
            kdec, v, (((0,), (0,)), ((), ())), preferred_element_type=F32)
        mu = jnp.mean(o, axis=-1, keepdims=True)
        var = jnp.mean(jnp.square(o - mu), axis=-1, keepdims=True)
        on = (o - mu) * lax.rsqrt(var + LN_EPS)
        g = g_ref[:, h * dv:(h + 1) * dv]
        r_ref[:, h * dv:(h + 1) * dv] = (g * jax.nn.sigmoid(g) * on).astype(r_ref.dtype)

    @pl.when(c == pl.num_programs(1) - 1)
    def _():
        sout_ref[0] = state_ref[...]


def _retention(u, row0, batch, seq, state0, cos, sin):
    _, heads, dk, dv = state0.shape
    chunk = min(seq, RET_CHUNK_MAX)
    nc = seq // chunk
    rb0 = row0 // chunk
    qw, vw = heads * dk, heads * dv

    def rows(b, c):
        return rb0 + b * nc + c

    kern = functools.partial(_retention_kernel, chunk=chunk, heads=heads, dk=dk, dv=dv)
    return pl.pallas_call(
        kern,
        grid=(batch, nc),
        in_specs=[pl.BlockSpec((chunk, qw), lambda b, c: (rows(b, c), 0)),
                  pl.BlockSpec((chunk, qw), lambda b, c: (rows(b, c), 1)),
                  pl.BlockSpec((chunk, vw), lambda b, c: (rows(b, c), (2 * qw) // vw)),
                  pl.BlockSpec((chunk, vw), lambda b, c: (rows(b, c), (2 * qw) // vw + 1)),
                  pl.BlockSpec((chunk, dk), lambda b, c: (c, 0)),
                  pl.BlockSpec((chunk, dk), lambda b, c: (c, 0)),
                  pl.BlockSpec((1, heads, dk, dv), lambda b, c: (b, 0, 0, 0))],
        out_specs=[pl.BlockSpec((chunk, vw), lambda b, c: (b * nc + c, 0)),
                   pl.BlockSpec((1, heads, dk, dv), lambda b, c: (b, 0, 0, 0))],
        out_shape=[jax.ShapeDtypeStruct((batch * seq, vw), BF16),
                   jax.ShapeDtypeStruct(state0.shape, F32)],
        scratch_shapes=[pltpu.VMEM((heads, dk, dv), F32)],
        compiler_params=_cparams(("arbitrary", "arbitrary"), 48),
        name="retention",
    )(u, u, u, u, cos, sin, state0)


def _conv_kernel(a_ref, b_ref, hist_ref, w_ref, bias_ref, lng_ref, lnb_ref, o_ref, cache_ref, s_ref, y_ref,
                 *, tt, width, off):
    t = pl.program_id(1)
    hl = width - 1
    ch = a_ref.shape[1]

    @pl.when(t == 0)
    def _():
        s_ref[off - hl:off, :] = hist_ref[0]

    @pl.when(t > 0)
    def _():
        s_ref[off - hl:off, :] = s_ref[off + tt - hl:off + tt, :]

    s_ref[off:off + tt, :] = a_ref[...] * jax.nn.sigmoid(b_ref[...])
    cache_ref[0] = s_ref[off + tt - hl:off + tt, :]

    rb = min(CONV_RB, tt)
    for cc in range(ch // LANES):
        cols = slice(cc * LANES, (cc + 1) * LANES)
        for r in range(tt // rb):
            acc = jnp.broadcast_to(bias_ref[:, cols], (rb, LANES))
            for j in range(width):
                st = off - hl + j + r * rb
                acc = acc + w_ref[j:j + 1, cols] * s_ref[st:st + rb, cols]
            y_ref[r * rb:(r + 1) * rb, cols] = acc

    y = y_ref[...]
    mu = jnp.mean(y, axis=-1, keepdims=True)
    var = jnp.mean(jnp.square(y - mu), axis=-1, keepdims=True)
    yn = (y - mu) * lax.rsqrt(var + LN_EPS) * lng_ref[...] + lnb_ref[...]
    o_ref[...] = (yn * jax.nn.sigmoid(yn)).astype(o_ref.dtype)


def _conv_module(u, row0, batch, seq, hist, conv_w, conv_b, ln_g, ln_b, a_col):
    width, ch = conv_w.shape
    tt = min(seq, CONV_TT_MAX)
    nt = seq // tt
    rb0 = row0 // tt
    off = 32
    assert width - 1 <= off and tt >= width - 1
    kern = functools.partial(_conv_kernel, tt=tt, width=width, off=off)
    vec = pl.BlockSpec((1, ch), lambda b, t: (0, 0))
    return pl.pallas_call(
        kern,
        grid=(batch, nt),
        in_specs=[pl.BlockSpec((tt, ch), lambda b, t: (rb0 + b * nt + t, a_col)),
                  pl.BlockSpec((tt, ch), lambda b, t: (rb0 + b * nt + t, a_col + 1)),
                  pl.BlockSpec((1, width - 1, ch), lambda b, t: (b, 0, 0)),
                  pl.BlockSpec((width, ch), lambda b, t: (0, 0)),
                  vec, vec, vec],
        out_specs=[pl.BlockSpec((tt, ch), lambda b, t: (b * nt + t, 0)),
                   pl.BlockSpec((1, width - 1, ch), lambda b, t: (b, 0, 0))],
        out_shape=[jax.ShapeDtypeStruct((batch * seq, ch), BF16),
                   jax.ShapeDtypeStruct((batch, width - 1, ch), F32)],
        scratch_shapes=[pltpu.VMEM((off + tt, ch), F32), pltpu.VMEM((tt, ch), F32)],
        compiler_params=_cparams(("arbitrary", "arbitrary"), 32),
        name="conv_module",
    )(u, u, hist, conv_w, conv_b.reshape(1, ch), ln_g.reshape(1, ch), ln_b.reshape(1, ch))


def _merge_kernel(r_ref, cv_ref, glr_ref, glc_ref, xa_ref, xb_ref, wr_ref, wc_ref, wo_ref, gf_ref, rw_ref, rb_ref,
                  h_ref, hnp_ref, lg_ref, *, a_steps):
    br = jnp.dot(r_ref[...], wr_ref[...], preferred_element_type=F32)
    bc = jnp.dot(cv_ref[...], wc_ref[...], preferred_element_type=F32)
    m = jax.nn.sigmoid(glr_ref[...]) * br + jax.nn.sigmoid(glc_ref[...]) * bc
    mo = jnp.dot(m.astype(BF16), wo_ref[...], preferred_element_type=F32)

    @pl.when(pl.program_id(0) < a_steps)
    def _():
        h_ref[...] = xa_ref[...] + mo

    @pl.when(pl.program_id(0) >= a_steps)
    def _():
        h_ref[...] = xb_ref[...] + mo

    h = h_ref[...]
    hn = h * lax.rsqrt(jnp.mean(h * h, axis=-1, keepdims=True) + RMS_EPS) * gf_ref[...]
    half = hn.shape[1] // 2
    a = hn[:, :half].astype(BF16)
    b = hn[:, half:].astype(BF16)
    lg_ref[...] = (jnp.dot(a, rw_ref[:half, :], preferred_element_type=F32)
                   + jnp.dot(b, rw_ref[half:, :], preferred_element_type=F32) + rb_ref[...])
    au = lax.bitcast_convert_type(a.astype(F32), U32)
    bu = lax.bitcast_convert_type(b.astype(F32), U32)
    hnp_ref[...] = au | (bu >> 16)


def _merge(r, cv, u, xa, xb, w_ret_o, w_conv_o, w_o, g_ffn, router_w, router_b, gate_col):
    (na, d), n = xa.shape, r.shape[0]
    ch = cv.shape[1]
    e_pad = router_w.shape[1]
    tm = MERGE_TM
    a_steps = na // tm

    def const(shape):
        return pl.BlockSpec(shape, lambda i: (0,) * len(shape), pipeline_mode=pl.Buffered(1))

    return pl.pallas_call(
        functools.partial(_merge_kernel, a_steps=a_steps),
        grid=(n // tm,),
        in_specs=[pl.BlockSpec((tm, r.shape[1]), lambda i: (i, 0)),
                  pl.BlockSpec((tm, ch), lambda i: (i, 0)),
                  pl.BlockSpec((tm, d), lambda i: (i, gate_col)),
                  pl.BlockSpec((tm, d), lambda i: (i, gate_col + 1)),
                  pl.BlockSpec((tm, d), lambda i: (jnp.minimum(i, a_steps - 1), 0)),
                  pl.BlockSpec((tm, d), lambda i: (jnp.maximum(i - a_steps, 0), 0)),
                  const(w_ret_o.shape), const(w_conv_o.shape), const(w_o.shape),
                  const((1, d)), const(router_w.shape), const((1, e_pad))],
        out_specs=[pl.BlockSpec((tm, d), lambda i: (i, 0)),
                   pl.BlockSpec((tm, d // 2), lambda i: (i, 0)),
                   pl.BlockSpec((tm, e_pad), lambda i: (i, 0))],
        out_shape=[jax.ShapeDtypeStruct((n, d), F32),
                   jax.ShapeDtypeStruct((n, d // 2), U32),
                   jax.ShapeDtypeStruct((n, e_pad), F32)],
        compiler_params=_cparams(("arbitrary",), 56),
        name="merge_router",
    )(r, cv, u, u, xa, xb, w_ret_o, w_conv_o, w_o, g_ffn.reshape(1, d), router_w, router_b)


def _dispatch_kernel(last_ref, nblk_ref, dest_ref, src_ref, out_ref, zero_ref, sem, zsem):
    ntok = src_ref.shape[0]
    nb_total = out_ref.shape[0] // MOE_TM

    def zero_block(row0):
        return pltpu.make_async_copy(zero_ref, out_ref.at[pl.ds(pl.multiple_of(row0, MOE_TM), MOE_TM)], zsem)

    @pl.when(pl.program_id(0) == 0)
    def _():
        zero_ref[...] = jnp.zeros_like(zero_ref)

        def each_block(fn):
            def expert(e, carry):
                @pl.when(last_ref[e] >= 0)
                def _():
                    fn(zero_block(last_ref[e]))
                return carry

            def tail(b, carry):
                fn(zero_block(b * MOE_TM))
                return carry

            lax.fori_loop(0, last_ref.shape[0], expert, 0)
            lax.fori_loop(nblk_ref[0], nb_total, tail, 0)

        each_block(lambda cp: cp.start())
        each_block(lambda cp: cp.wait())

    def each_row(fn):
        def body(t, carry):
            for k in range(TOP_K):
                fn(pltpu.make_async_copy(src_ref.at[pl.ds(t, 1)],
                                         out_ref.at[pl.ds(dest_ref[0, 0, t * TOP_K + k], 1)], sem))
            return carry

        lax.fori_loop(0, ntok, body, 0)

    each_row(lambda cp: cp.start())
    each_row(lambda cp: cp.wait())


def _dispatch(hnp, dest, last_blk_row, nblk, rows_pad):
    n, dw = hnp.shape
    tok = DISPATCH_TOK
    nsteps = n // tok
    grid_spec = pltpu.PrefetchScalarGridSpec(
        num_scalar_prefetch=2,
        grid=(nsteps,),
        in_specs=[pl.BlockSpec((1, 1, tok * TOP_K), lambda i, *_: (i, 0, 0), memory_space=pltpu.SMEM),
                  pl.BlockSpec((tok, dw), lambda i, *_: (i, 0))],
        out_specs=pl.BlockSpec(memory_space=pl.ANY),
        scratch_shapes=[pltpu.VMEM((MOE_TM, dw), U32), pltpu.SemaphoreType.DMA(()), pltpu.SemaphoreType.DMA(())],
    )
    return pl.pallas_call(
        _dispatch_kernel,
        grid_spec=grid_spec,
        out_shape=jax.ShapeDtypeStruct((rows_pad, dw), U32),
        compiler_params=_cparams(("arbitrary",), 32),
        name="moe_dispatch",
    )(last_blk_row, nblk, dest.reshape(nsteps, 1, tok * TOP_K), hnp)


def _unpack_bf16_pair(w):
    hi = lax.bitcast_convert_type(w & jnp.uint32(0xFFFF0000), F32).astype(BF16)
    lo = lax.bitcast_convert_type(w << 16, F32).astype(BF16)
    return hi, lo


def _moe1_kernel(be_ref, nb_ref, x_ref, wg_ref, wu_ref, bg_ref, bu_ref, o_ref, wbf_ref):
    b = pl.program_id(1)
    tf = wg_ref.shape[2]
    valid = b < nb_ref[0]
    first = jnp.logical_or(b == 0, be_ref[b] != be_ref[jnp.maximum(b - 1, 0)])

    @pl.when(jnp.logical_and(valid, first))
    def _():
        wbf_ref[:, :tf] = wg_ref[0].astype(BF16)
        wbf_ref[:, tf:] = wu_ref[0].astype(BF16)

    @pl.when(valid)
    def _():
        half = x_ref.shape[1]

        def body(r, carry):
            rows = pl.ds(pl.multiple_of(r * MOE_SUB, MOE_SUB), MOE_SUB)
            hi, lo = _unpack_bf16_pair(x_ref[rows, :])
            gu = (jnp.dot(hi, wbf_ref[:half, :], preferred_element_type=F32)
                  + jnp.dot(lo, wbf_ref[half:, :], preferred_element_type=F32))
            g = jnp.minimum(gu[:, :tf] + bg_ref[0], SWIGLU_LIMIT)
            u = jnp.clip(gu[:, tf:] + bu_ref[0], -SWIGLU_LIMIT, SWIGLU_LIMIT)
            o_ref[rows, :] = ((u + 1.0) * (g * jax.nn.sigmoid(SWIGLU_ALPHA * g))).astype(o_ref.dtype)
            return carry

        lax.fori_loop(0, x_ref.shape[0] // MOE_SUB, body, 0)

    @pl.when(jnp.logical_not(valid))
    def _():
        o_ref[...] = jnp.zeros_like(o_ref)


def _moe_gate_up(xs, blk_e, nblk, w_gate_up, b_gate_up):
    rows_pad, dw = xs.shape
    e, d, f2 = w_gate_up.shape
    f = f2 // 2
    tf = MOE1_TF
    nb = rows_pad // MOE_TM
    nj = f // tf

    def xrow(j, b, be, nbr):
        return (jnp.minimum(b, nbr[0] - 1), 0)

    grid_spec = pltpu.PrefetchScalarGridSpec(
        num_scalar_prefetch=2,
        grid=(nj, nb),
        in_specs=[pl.BlockSpec((MOE_TM, dw), xrow),
                  pl.BlockSpec((1, d, tf), lambda j, b, be, nbr: (be[b], 0, j)),
                  pl.BlockSpec((1, d, tf), lambda j, b, be, nbr: (be[b], 0, nj + j)),
                  pl.BlockSpec((1, 1, tf), lambda j, b, be, nbr: (be[b], 0, j)),
                  pl.BlockSpec((1, 1, tf), lambda j, b, be, nbr: (be[b], 0, nj + j))],
        out_specs=pl.BlockSpec((MOE_TM, tf), lambda j, b, be, nbr: (b, j)),
        scratch_shapes=[pltpu.VMEM((d, 2 * tf), BF16)],
    )
    return pl.pallas_call(
        _moe1_kernel,
        grid_spec=grid_spec,
        out_shape=jax.ShapeDtypeStruct((rows_pad, f), BF16),
        compiler_params=_cparams(("arbitrary", "arbitrary"), 48),
        name="moe_gate_up",
    )(blk_e, nblk, xs, w_gate_up, w_gate_up, b_gate_up.reshape(e, 1, f2), b_gate_up.reshape(e, 1, f2))


def _moe2_kernel(be_ref, nb_ref, x_ref, w_ref, bias_ref, o_ref, wbf_ref):
    b = pl.program_id(1)
    valid = b < nb_ref[0]
    first = jnp.logical_or(b == 0, be_ref[b] != be_ref[jnp.maximum(b - 1, 0)])

    @pl.when(jnp.logical_and(valid, first))
    def _():
        wbf_ref[...] = w_ref[0].astype(BF16)

    @pl.when(valid)
    def _():
        def body(r, carry):
            rows = pl.ds(pl.multiple_of(r * MOE_SUB, MOE_SUB), MOE_SUB)
            o_ref[rows, :] = jnp.dot(x_ref[rows, :], wbf_ref[...], preferred_element_type=F32) + bias_ref[0]
            return carry

        lax.fori_loop(0, x_ref.shape[0] // MOE_SUB, body, 0)

    @pl.when(jnp.logical_not(valid))
    def _():
        o_ref[...] = jnp.zeros_like(o_ref)


def _moe_down(hdn, blk_e, nblk, w_down, b_down):
    rows_pad, f = hdn.shape
    e, _, d = w_down.shape
    tn = MOE2_TN
    nb = rows_pad // MOE_TM
    grid_spec = pltpu.PrefetchScalarGridSpec(
        num_scalar_prefetch=2,
        grid=(d // tn, nb),
        in_specs=[pl.BlockSpec((MOE_TM, f), lambda j, b, be, nbr: (jnp.minimum(b, nbr[0] - 1), 0)),
                  pl.BlockSpec((1, f, tn), lambda j, b, be, nbr: (be[b], 0, j)),
                  pl.BlockSpec((1, 1, tn), lambda j, b, be, nbr: (be[b], 0, j))],
        out_specs=pl.BlockSpec((MOE_TM, tn), lambda j, b, be, nbr: (b, j)),
        scratch_shapes=[pltpu.VMEM((f, tn), BF16)],
    )
    return pl.pallas_call(
        _moe2_kernel,
        grid_spec=grid_spec,
        out_shape=jax.ShapeDtypeStruct((rows_pad, d), F32),
        compiler_params=_cparams(("arbitrary", "arbitrary"), 48),
        name="moe_down",
    )(blk_e, nblk, hdn, w_down, b_down.reshape(e, 1, d))


def _combine_kernel(dest_ref, dest_next_ref, y_ref, gate_ref, h_ref, g_ref, o_ref, buf_ref, sem):
    i = pl.program_id(0)
    ntok = h_ref.shape[0]
    slot = i % 2

    def each_row(d_ref, s, fn):
        def body(t, carry):
            for k in range(TOP_K):
                fn(pltpu.make_async_copy(y_ref.at[pl.ds(d_ref[0, 0, t * TOP_K + k], 1)],
                                         buf_ref.at[s, k, pl.ds(t, 1)], sem.at[s]))
            return carry

        lax.fori_loop(0, ntok, body, 0)

    @pl.when(i == 0)
    def _():
        each_row(dest_ref, 0, lambda cp: cp.start())

    @pl.when(i + 1 < pl.num_programs(0))
    def _():
        each_row(dest_next_ref, 1 - slot, lambda cp: cp.start())

    each_row(dest_ref, slot, lambda cp: cp.wait())
    gate = gate_ref[...]
    acc = h_ref[...]
    for k in range(TOP_K):
        acc = acc + gate[:, k:k + 1] * buf_ref[slot, k]
    o_ref[...] = acc * lax.rsqrt(jnp.mean(acc * acc, axis=-1, keepdims=True) + RMS_EPS) * g_ref[...]


def _combine(yb, dest, gate, h, g_final):
    n, d = h.shape
    tok = COMBINE_TOK
    nsteps = n // tok
    dest3 = dest.reshape(nsteps, 1, tok * TOP_K)
    return pl.pallas_call(
        _combine_kernel,
        grid=(nsteps,),
        in_specs=[pl.BlockSpec((1, 1, tok * TOP_K), lambda i: (i, 0, 0), memory_space=pltpu.SMEM),
                  pl.BlockSpec((1, 1, tok * TOP_K), lambda i: (jnp.minimum(i + 1, nsteps - 1), 0, 0),
                               memory_space=pltpu.SMEM),
                  pl.BlockSpec(memory_space=pl.ANY),
                  pl.BlockSpec((tok, TOP_K), lambda i: (i, 0)),
                  pl.BlockSpec((tok, d), lambda i: (i, 0)),
                  pl.BlockSpec((1, d), lambda i: (0, 0))],
        out_specs=pl.BlockSpec((tok, d), lambda i: (i, 0)),
        out_shape=jax.ShapeDtypeStruct((n, d), F32),
        scratch_shapes=[pltpu.VMEM((2, TOP_K, tok, d), F32), pltpu.SemaphoreType.DMA((2,))],
        compiler_params=_cparams(("arbitrary",), 32),
        name="moe_combine",
    )(dest3, dest3, yb, gate, h, g_final.reshape(1, d))


def _routing(logits, n_experts, rows_pad):
    top_val, top_idx = lax.top_k(logits, TOP_K)
    gate = jax.nn.softmax(top_val, axis=-1)
    flat_e = top_idx.reshape(-1).astype(jnp.int32)
    onehot = (flat_e[:, None] == jnp.arange(n_experts, dtype=jnp.int32)[None, :]).astype(jnp.int32)
    before = jnp.cumsum(onehot, axis=0) - onehot
    rank = jnp.sum(before * onehot, axis=1)
    counts = jnp.sum(onehot, axis=0)
    nblk_e = (counts + MOE_TM - 1) // MOE_TM
    blk_end = jnp.cumsum(nblk_e)
    pstart = (blk_end - nblk_e) * MOE_TM
    dest = jnp.sum(onehot * pstart[None, :], axis=1) + rank
    nblk = blk_end[-1]
    nb = rows_pad // MOE_TM
    bidx = jnp.minimum(jnp.arange(nb, dtype=jnp.int32), nblk - 1)
    blk_e = jnp.sum((bidx[:, None] >= blk_end[None, :]).astype(jnp.int32), axis=1)
    last_blk_row = jnp.where(nblk_e > 0, (blk_end - 1) * MOE_TM, -1)
    return (gate, dest.astype(jnp.int32), blk_e.astype(jnp.int32), nblk.reshape(1).astype(jnp.int32),
            last_blk_row.astype(jnp.int32))


def _rope_tables(pos, dk):
    half = dk // 2
    freq = ROPE_BASE ** (-jnp.arange(half, dtype=F32) / half)
    ang = pos.astype(F32)[:, None] * freq[None, :]
    cos, sin = jnp.cos(ang), jnp.sin(ang)
    return jnp.concatenate([cos, cos], axis=-1), jnp.concatenate([-sin, sin], axis=-1)


def kernel(x_prompt, x_sample, state_ret, cache_conv, norm_mix_g, w_in, w_ret_o, conv_dw_w, conv_dw_b, conv_ln_g, conv_ln_b, w_conv_o, w_o, norm_ffn_g, router_w, router_b, w_gate_up, b_gate_up, w_down, b_down, norm_final_g):
    bp, tp, d = x_prompt.shape
    bs, ts, _ = x_sample.shape
    depth, _, heads, dk, dv = state_ret.shape
    ch = cache_conv.shape[-1]
    n_experts = router_w.shape[-1]
    np_rows, ns_rows = bp * tp, bs * ts
    n = np_rows + ns_rows
    qw, vw = heads * dk, heads * dv
    a_col = (2 * qw + 2 * vw) // ch
    gate_col = (2 * qw + 2 * vw + 2 * ch) // d
    rows_pad = (n * TOP_K // MOE_TM + n_experts) * MOE_TM

    cos_p, sin_p = _rope_tables(jnp.arange(tp, dtype=jnp.int32), dk)
    cos_s, sin_s = _rope_tables(PAST_LEN + jnp.arange(ts, dtype=jnp.int32), dk)

    assert depth == 1, "the final norm is fused into the layer's last stage"
    xa, xb = x_prompt.reshape(np_rows, d), x_sample.reshape(ns_rows, d)
    xn = _rmsnorm_bf16(xa, xb, norm_mix_g[0])
    u = _input_projection(xn, w_in[0])

    zero_ret = jnp.zeros((bp,) + state_ret.shape[2:], F32)
    zero_conv = jnp.zeros((bp,) + cache_conv.shape[2:], F32)
    r_p, s_p = _retention(u, 0, bp, tp, zero_ret, cos_p, sin_p)
    r_s, s_s = _retention(u, np_rows, bs, ts, state_ret[0], cos_s, sin_s)
    c_p, cc_p = _conv_module(u, 0, bp, tp, zero_conv, conv_dw_w[0], conv_dw_b[0], conv_ln_g[0], conv_ln_b[0], a_col)
    c_s, cc_s = _conv_module(u, np_rows, bs, ts, cache_conv[0], conv_dw_w[0], conv_dw_b[0], conv_ln_g[0],
                             conv_ln_b[0], a_col)
    r = jnp.concatenate([r_p, r_s], axis=0)
    cv = jnp.concatenate([c_p, c_s], axis=0)

    e_pad = -(-n_experts // LANES) * LANES
    rw = jnp.pad(router_w[0], ((0, 0), (0, e_pad - n_experts))).astype(BF16)
    rb = jnp.pad(router_b[0], (0, e_pad - n_experts)).reshape(1, e_pad)
    hmid, hnp, logits = _merge(r, cv, u, xa, xb, w_ret_o[0].astype(BF16), w_conv_o[0].astype(BF16),
                               w_o[0].astype(BF16), norm_ffn_g[0], rw, rb, gate_col)

    gate, dest, blk_e, nblk, last_blk_row = _routing(logits[:, :n_experts], n_experts, rows_pad)
    xs = _dispatch(hnp, dest, last_blk_row, nblk, rows_pad)
    hdn = _moe_gate_up(xs, blk_e, nblk, w_gate_up[0], b_gate_up[0])
    yb = _moe_down(hdn, blk_e, nblk, w_down[0], b_down[0])
    y = _combine(yb, dest, gate, hmid, norm_final_g)
    y_prompt = y[:np_rows].reshape(bp, tp, d)
    y_sample = y[np_rows:].reshape(bs, ts, d)
    return (y_prompt, y_sample, s_p[None], cc_p[None], s_s[None], cc_s[None])
```

```python
import functools
import math

import jax
import jax.numpy as jnp
from jax import lax
from jax.experimental import pallas as pl
from jax.experimental.pallas import tpu as pltpu

TOP_K = 4
PAST_LEN = 1024
ROPE_BASE = 10000.0
RMS_EPS = 1e-6
LN_EPS = 1e-5
SWIGLU_LIMIT = 7.0
SWIGLU_ALPHA = 1.702

F32 = jnp.float32
BF16 = jnp.bfloat16
U32 = jnp.uint32

V7X_VMEM_BYTES = 64 * 1024 * 1024
LANES = 128

NORM_ROWS = 512
PROJ_TM = 1024
PROJ_TN = 1024
PROJ_SUB = 256
RET_CHUNK_MAX = 256
CONV_TT_MAX = 256
CONV_RB = 128
MERGE_TM = 256
MOE_TM = 1024
MOE1_TF = 512
MOE2_TN = 1024
MOE_SUB = 256
DISPATCH_TOK = 256
COMBINE_TOK = 128


def _cparams(sem, vmem_mb):
    return pltpu.CompilerParams(dimension_semantics=sem, vmem_limit_bytes=vmem_mb * 1024 * 1024)


def _rmsnorm_kernel(xa_ref, xb_ref, g_ref, o_ref, *, a_steps):
    def norm(x_ref):
        x = x_ref[...]
        ms = jnp.mean(x * x, axis=-1, keepdims=True)
        o_ref[...] = (x * lax.rsqrt(ms + RMS_EPS) * g_ref[...]).astype(o_ref.dtype)

    @pl.when(pl.program_id(0) < a_steps)
    def _():
        norm(xa_ref)

    @pl.when(pl.program_id(0) >= a_steps)
    def _():
        norm(xb_ref)


def _rmsnorm_bf16(xa, xb, g):
    (na, d), nb = xa.shape, xb.shape[0]
    a_steps = na // NORM_ROWS
    return pl.pallas_call(
        functools.partial(_rmsnorm_kernel, a_steps=a_steps),
        grid=((na + nb) // NORM_ROWS,),
        in_specs=[pl.BlockSpec((NORM_ROWS, d), lambda i: (jnp.minimum(i, a_steps - 1), 0)),
                  pl.BlockSpec((NORM_ROWS, d), lambda i: (jnp.maximum(i - a_steps, 0), 0)),
                  pl.BlockSpec((1, d), lambda i: (0, 0))],
        out_specs=pl.BlockSpec((NORM_ROWS, d), lambda i: (i, 0)),
        out_shape=jax.ShapeDtypeStruct((na + nb, d), BF16),
        compiler_params=_cparams(("arbitrary",), 32),
        name="rmsnorm_bf16",
    )(xa, xb, g.reshape(1, d))


def _inproj_kernel(x_ref, w_ref, o_ref, wbf_ref):
    @pl.when(pl.program_id(1) == 0)
    def _():
        wbf_ref[...] = w_ref[...].astype(BF16)

    def body(r, carry):
        rows = pl.ds(pl.multiple_of(r * PROJ_SUB, PROJ_SUB), PROJ_SUB)
        o_ref[rows, :] = jnp.dot(x_ref[rows, :], wbf_ref[...], preferred_element_type=F32)
        return carry

    lax.fori_loop(0, PROJ_TM // PROJ_SUB, body, 0)


def _input_projection(xn, w):
    n, d = xn.shape
    width = w.shape[1]
    return pl.pallas_call(
        _inproj_kernel,
        grid=(width // PROJ_TN, n // PROJ_TM),
        in_specs=[pl.BlockSpec((PROJ_TM, d), lambda j, i: (i, 0)),
                  pl.BlockSpec((d, PROJ_TN), lambda j, i: (0, j))],
        out_specs=pl.BlockSpec((PROJ_TM, PROJ_TN), lambda j, i: (i, j)),
        out_shape=jax.ShapeDtypeStruct((n, width), F32),
        scratch_shapes=[pltpu.VMEM((d, PROJ_TN), BF16)],
        compiler_params=_cparams(("arbitrary", "arbitrary"), 48),
        name="input_projection",
    )(xn, w)


def _retention_kernel(q_ref, k_ref, v_ref, g_ref, cos_ref, sin_ref, s0_ref, r_ref, sout_ref, state_ref,
                      *, chunk, heads, dk, dv):
    c = pl.program_id(1)

    @pl.when(c == 0)
    def _():
        state_ref[...] = s0_ref[0]

    cos = cos_ref[...]
    sin = sin_ref[...]
    li = lax.broadcasted_iota(jnp.int32, (chunk, chunk), 0)
    mi = lax.broadcasted_iota(jnp.int32, (chunk, chunk), 1)
    rel = (li - mi).astype(F32)
    row = lax.broadcasted_iota(jnp.int32, (chunk, 1), 0).astype(F32)
    for h in range(heads):
        lg = math.log1p(-(2.0 ** (-5.0 - h)))
        q = q_ref[:, h * dk:(h + 1) * dk]
        k = k_ref[:, h * dk:(h + 1) * dk]
        v = v_ref[:, h * dv:(h + 1) * dv].astype(BF16)
        qr = (q * cos + pltpu.roll(q, dk // 2, 1) * sin).astype(BF16)
        kr = (k * cos + pltpu.roll(k, dk // 2, 1) * sin) * (dk ** -0.5)
        decay = jnp.where(rel >= 0, jnp.exp(lg * jnp.maximum(rel, 0.0)), 0.0)
        scores = lax.dot_general(qr, kr.astype(BF16), (((1,), (1,)), ((), ())),
                                 preferred_element_type=F32) * decay
        intra = jnp.dot(scores.astype(BF16), v, preferred_element_type=F32)
        st = state_ref[h]
        cross = jnp.dot(qr, st.astype(BF16), preferred_element_type=F32) * jnp.exp(lg * (row + 1.0))
        o = intra + cross
        kdec = (kr * jnp.exp(lg * (chunk - 1.0 - row))).astype(BF16)
        state_ref[h] = math.exp(lg * chunk) * st + lax.dot_general(
            kdec, v, (((0,), (0,)), ((), ())), preferred_element_type=F32)
        mu = jnp.mean(o, axis=-1, keepdims=True)
        var = jnp.mean(jnp.square(o - mu), axis=-1, keepdims=True)
        on = (o - mu) * lax.rsqrt(var + LN_EPS)
        g = g_ref[:, h * dv:(h + 1) * dv]
        r_ref[:, h * dv:(h + 1) * dv] = (g * jax.nn.sigmoid(g) * on).astype(r_ref.dtype)

    @pl.when(c == pl.num_programs(1) - 1)
    def _():
        sout_ref[0] = state_ref[...]


def _retention(u, row0, batch, seq, state0, cos, sin):
    _, heads, dk, dv = state0.shape
    chunk = min(seq, RET_CHUNK_MAX)
    nc = seq // chunk
    rb0 = row0 // chunk
    qw, vw = heads * dk, heads * dv

    def rows(b, c):
        return rb0 + b * nc + c

    kern = functools.partial(_retention_kernel, chunk=chunk, heads=heads, dk=dk, dv=dv)
    return pl.pallas_call(
        kern,
        grid=(batch, nc),
        in_specs=[pl.BlockSpec((chunk, qw), lambda b, c: (rows(b, c), 0)),
                  pl.BlockSpec((chunk, qw), lambda b, c: (rows(b, c), 1)),
                  pl.BlockSpec((chunk, vw), lambda b, c: (rows(b, c), (2 * qw) // vw)),
                  pl.BlockSpec((chunk, vw), lambda b, c: (rows(b, c), (2 * qw) // vw + 1)),
                  pl.BlockSpec((chunk, dk), lambda b, c: (c, 0)),
                  pl.BlockSpec((chunk, dk), lambda b, c: (c, 0)),
                  pl.BlockSpec((1, heads, dk, dv), lambda b, c: (b, 0, 0, 0))],
        out_specs=[pl.BlockSpec((chunk, vw), lambda b, c: (b * nc + c, 0)),
                   pl.BlockSpec((1, heads, dk, dv), lambda b, c: (b, 0, 0, 0))],
        out_shape=[jax.ShapeDtypeStruct((batch * seq, vw), BF16),
                   jax.ShapeDtypeStruct(state0.shape, F32)],
        scratch_shapes=[pltpu.VMEM((heads, dk, dv), F32)],
        compiler_params=_cparams(("arbitrary", "arbitrary"), 48),
        name="retention",
    )(u, u, u, u, cos, sin, state0)


def _conv_kernel(a_ref, b_ref, hist_ref, w_ref, bias_ref, lng_ref, lnb_ref, o_ref, cache_ref, s_ref, y_ref,
                 *, tt, width, off):
    t = pl.program_id(1)
    hl = width - 1
    ch = a_ref.shape[1]

    @pl.when(t == 0)
    def _():
        s_ref[off - hl:off, :] = hist_ref[0]

    @pl.when(t > 0)
    def _():
        s_ref[off - hl:off, :] = s_ref[off + tt - hl:off + tt, :]

    s_ref[off:off + tt, :] = a_ref[...] * jax.nn.sigmoid(b_ref[...])
    cache_ref[0] = s_ref[off + tt - hl:off + tt, :]

    rb = min(CONV_RB, tt)
    for cc in range(ch // LANES):
        cols = slice(cc * LANES, (cc + 1) * LANES)
        for r in range(tt // rb):
            acc = jnp.broadcast_to(bias_ref[:, cols], (rb, LANES))
            for j in range(width):
                st = off - hl + j + r * rb
                acc = acc + w_ref[j:j + 1, cols] * s_ref[st:st + rb, cols]
            y_ref[r * rb:(r + 1) * rb, cols] = acc

    y = y_ref[...]
    mu = jnp.mean(y, axis=-1, keepdims=True)
    var = jnp.mean(jnp.square(y - mu), axis=-1, keepdims=True)
    yn = (y - mu) * lax.rsqrt(var + LN_EPS) * lng_ref[...] + lnb_ref[...]
    o_ref[...] = (yn * jax.nn.sigmoid(yn)).astype(o_ref.dtype)


def _conv_module(u, row0, batch, seq, hist, conv_w, conv_b, ln_g, ln_b, a_col):
    width, ch = conv_w.shape
    tt = min(seq, CONV_TT_MAX)
    nt = seq // tt
    rb0 = row0 // tt
    off = 32
    assert width - 1 <= off and tt >= width - 1
    kern = functools.partial(_conv_kernel, tt=tt, width=width, off=off)
    vec = pl.BlockSpec((1, ch), lambda b, t: (0, 0))
    return pl.pallas_call(
        kern,
        grid=(batch, nt),
        in_specs=[pl.BlockSpec((tt, ch), lambda b, t: (rb0 + b * nt + t, a_col)),
                  pl.BlockSpec((tt, ch), lambda b, t: (rb0 + b * nt + t, a_col + 1)),
                  pl.BlockSpec((1, width - 1, ch), lambda b, t: (b, 0, 0)),
                  pl.BlockSpec((width, ch), lambda b, t: (0, 0)),
                  vec, vec, vec],
        out_specs=[pl.BlockSpec((tt, ch), lambda b, t: (b * nt + t, 0)),
                   pl.BlockSpec((1, width - 1, ch), lambda b, t: (b, 0, 0))],
        out_shape=[jax.ShapeDtypeStruct((batch * seq, ch), BF16),
                   jax.ShapeDtypeStruct((batch, width - 1, ch), F32)],
        scratch_shapes=[pltpu.VMEM((off + tt, ch), F32), pltpu.VMEM((tt, ch), F32)],
        compiler_params=_cparams(("arbitrary", "arbitrary"), 32),
        name="conv_module",
    )(u, u, hist, conv_w, conv_b.reshape(1, ch), ln_g.reshape(1, ch), ln_b.reshape(1, ch))


def _merge_kernel(r_ref, cv_ref, glr_ref, glc_ref, xa_ref, xb_ref, wr_ref, wc_ref, wo_ref, gf_ref, rw_ref, rb_ref,
                  h_ref, hnp_ref, lg_ref, *, a_steps):
    br = jnp.dot(r_ref[...], wr_ref[...], preferred_element_type=F32)
    bc = jnp.dot(cv_ref[...], wc_ref[...], preferred_element_type=F32)
    m = jax.nn.sigmoid(glr_ref[...]) * br + jax.nn.sigmoid(glc_ref[...]) * bc
    mo = jnp.dot(m.astype(BF16), wo_ref[...], preferred_element_type=F32)

    @pl.when(pl.program_id(0) < a_steps)
    def _():
        h_ref[...] = xa_ref[...] + mo

    @pl.when(pl.program_id(0) >= a_steps)
    def _():
        h_ref[...] = xb_ref[...] + mo

    h = h_ref[...]
    hn = h * lax.rsqrt(jnp.mean(h * h, axis=-1, keepdims=True) + RMS_EPS) * gf_ref[...]
    half = hn.shape[1] // 2
    a = hn[:, :half].astype(BF16)
    b = hn[:, half:].astype(BF16)
    lg_ref[...] = (jnp.dot(a, rw_ref[:half, :], preferred_element_type=F32)
                   + jnp.dot(b, rw_ref[half:, :], preferred_element_type=F32) + rb_ref[...])
    au = lax.bitcast_convert_type(a.astype(F32), U32)
    bu = lax.bitcast_convert_type(b.astype(F32), U32)
    hnp_ref[...] = au | (bu >> 16)


def _merge(r, cv, u, xa, xb, w_ret_o, w_conv_o, w_o, g_ffn, router_w, router_b, gate_col):
    (na, d), n = xa.shape, r.shape[0]
    ch = cv.shape[1]
    e_pad = router_w.shape[1]
    tm = MERGE_TM
    a_steps = na // tm

    def const(shape):
        return pl.BlockSpec(shape, lambda i: (0,) * len(shape), pipeline_mode=pl.Buffered(1))

    return pl.pallas_call(
        functools.partial(_merge_kernel, a_steps=a_steps),
        grid=(n // tm,),
        in_specs=[pl.BlockSpec((tm, r.shape[1]), lambda i: (i, 0)),
                  pl.BlockSpec((tm, ch), lambda i: (i, 0)),
                  pl.BlockSpec((tm, d), lambda i: (i, gate_col)),
                  pl.BlockSpec((tm, d), lambda i: (i, gate_col + 1)),
                  pl.BlockSpec((tm, d), lambda i: (jnp.minimum(i, a_steps - 1), 0)),
                  pl.BlockSpec((tm, d), lambda i: (jnp.maximum(i - a_steps, 0), 0)),
                  const(w_ret_o.shape), const(w_conv_o.shape), const(w_o.shape),
                  const((1, d)), const(router_w.shape), const((1, e_pad))],
        out_specs=[pl.BlockSpec((tm, d), lambda i: (i, 0)),
                   pl.BlockSpec((tm, d // 2), lambda i: (i, 0)),
                   pl.BlockSpec((tm, e_pad), lambda i: (i, 0))],
        out_shape=[jax.ShapeDtypeStruct((n, d), F32),
                   jax.ShapeDtypeStruct((n, d // 2), U32),
                   jax.ShapeDtypeStruct((n, e_pad), F32)],
        compiler_params=_cparams(("arbitrary",), 56),
        name="merge_router",
    )(r, cv, u, u, xa, xb, w_ret_o, w_conv_o, w_o, g_ffn.reshape(1, d), router_w, router_b)


def _dispatch_kernel(last_ref, nblk_ref, dest_ref, src_ref, out_ref, zero_ref, sem, zsem):
    ntok = src_ref.shape[0]
    nb_total = out_ref.shape[0] // MOE_TM

    def zero_block(row0):
        return pltpu.make_async_copy(zero_ref, out_ref.at[pl.ds(pl.multiple_of(row0, MOE_TM), MOE_TM)], zsem)

    @pl.when(pl.program_id(0) == 0)
    def _():
        zero_ref[...] = jnp.zeros_like(zero_ref)

        def each_block(fn):
            def expert(e, carry):
                @pl.when(last_ref[e] >= 0)
                def _():
                    fn(zero_block(last_ref[e]))
                return carry

            def tail(b, carry):
                fn(zero_block(b * MOE_TM))
                return carry

            lax.fori_loop(0, last_ref.shape[0], expert, 0)
            lax.fori_loop(nblk_ref[0], nb_total, tail, 0)

        each_block(lambda cp: cp.start())
        each_block(lambda cp: cp.wait())

    def each_row(fn):
        def body(t, carry):
            for k in range(TOP_K):
                fn(pltpu.make_async_copy(src_ref.at[pl.ds(t, 1)],
                                         out_ref.at[pl.ds(dest_ref[0, 0, t * TOP_K + k], 1)], sem))
            return carry

        lax.fori_loop(0, ntok, body, 0)

    each_row(lambda cp: cp.start())
    each_row(lambda cp: cp.wait())


def _dispatch(hnp, dest, last_blk_row, nblk, rows_pad):
    n, dw = hnp.shape
    tok = DISPATCH_TOK
    nsteps = n // tok
    grid_spec = pltpu.PrefetchScalarGridSpec(
        num_scalar_prefetch=2,
        grid=(nsteps,),
        in_specs=[pl.BlockSpec((1, 1, tok * TOP_K), lambda i, *_: (i, 0, 0), memory_space=pltpu.SMEM),
                  pl.BlockSpec((tok, dw), lambda i, *_: (i, 0))],
        out_specs=pl.BlockSpec(memory_space=pl.ANY),
        scratch_shapes=[pltpu.VMEM((MOE_TM, dw), U32), pltpu.SemaphoreType.DMA(()), pltpu.SemaphoreType.DMA(())],
    )
    return pl.pallas_call(
        _dispatch_kernel,
        grid_spec=grid_spec,
        out_shape=jax.ShapeDtypeStruct((rows_pad, dw), U32),
        compiler_params=_cparams(("arbitrary",), 32),
        name="moe_dispatch",
    )(last_blk_row, nblk, dest.reshape(nsteps, 1, tok * TOP_K), hnp)


def _unpack_bf16_pair(w):
    hi = lax.bitcast_convert_type(w & jnp.uint32(0xFFFF0000), F32).astype(BF16)
    lo = lax.bitcast_convert_type(w << 16, F32).astype(BF16)
    return hi, lo


def _moe1_kernel(be_ref, nb_ref, ns_ref, x_ref, wg_ref, wu_ref, bg_ref, bu_ref, o_ref, wbf_ref):
    b = pl.program_id(1)
    tf = wg_ref.shape[2]
    valid = b < nb_ref[0]
    first = jnp.logical_or(b == 0, be_ref[b] != be_ref[jnp.maximum(b - 1, 0)])

    @pl.when(jnp.logical_and(valid, first))
    def _():
        wbf_ref[:, :tf] = wg_ref[0].astype(BF16)
        wbf_ref[:, tf:] = wu_ref[0].astype(BF16)

    @pl.when(valid)
    def _():
        half = x_ref.shape[1]

        def body(r, carry):
            rows = pl.ds(pl.multiple_of(r * MOE_SUB, MOE_SUB), MOE_SUB)
            hi, lo = _unpack_bf16_pair(x_ref[rows, :])
            gu = (jnp.dot(hi, wbf_ref[:half, :], preferred_element_type=F32)
                  + jnp.dot(lo, wbf_ref[half:, :], preferred_element_type=F32))
            g = jnp.minimum(gu[:, :tf] + bg_ref[0], SWIGLU_LIMIT)
            u = jnp.clip(gu[:, tf:] + bu_ref[0], -SWIGLU_LIMIT, SWIGLU_LIMIT)
            o_ref[rows, :] = ((u + 1.0) * (g * jax.nn.sigmoid(SWIGLU_ALPHA * g))).astype(o_ref.dtype)
            return carry

        def pad(r, carry):
            rows = pl.ds(pl.multiple_of(r * MOE_SUB, MOE_SUB), MOE_SUB)
            o_ref[rows, :] = jnp.zeros((MOE_SUB, o_ref.shape[1]), o_ref.dtype)
            return carry

        lax.fori_loop(0, ns_ref[b], body, 0)
        lax.fori_loop(ns_ref[b], x_ref.shape[0] // MOE_SUB, pad, 0)

    @pl.when(jnp.logical_not(valid))
    def _():
        o_ref[...] = jnp.zeros_like(o_ref)


def _moe_gate_up(xs, blk_e, nblk, nsub, w_gate_up, b_gate_up):
    rows_pad, dw = xs.shape
    e, d, f2 = w_gate_up.shape
    f = f2 // 2
    tf = MOE1_TF
    nb = rows_pad // MOE_TM
    nj = f // tf

    def xrow(j, b, be, nbr, ns):
        return (jnp.minimum(b, nbr[0] - 1), 0)

    grid_spec = pltpu.PrefetchScalarGridSpec(
        num_scalar_prefetch=3,
        grid=(nj, nb),
        in_specs=[pl.BlockSpec((MOE_TM, dw), xrow),
                  pl.BlockSpec((1, d, tf), lambda j, b, be, nbr, ns: (be[b], 0, j)),
                  pl.BlockSpec((1, d, tf), lambda j, b, be, nbr, ns: (be[b], 0, nj + j)),
                  pl.BlockSpec((1, 1, tf), lambda j, b, be, nbr, ns: (be[b], 0, j)),
                  pl.BlockSpec((1, 1, tf), lambda j, b, be, nbr, ns: (be[b], 0, nj + j))],
        out_specs=pl.BlockSpec((MOE_TM, tf), lambda j, b, be, nbr, ns: (b, j)),
        scratch_shapes=[pltpu.VMEM((d, 2 * tf), BF16)],
    )
    return pl.pallas_call(
        _moe1_kernel,
        grid_spec=grid_spec,
        out_shape=jax.ShapeDtypeStruct((rows_pad, f), BF16),
        compiler_params=_cparams(("arbitrary", "arbitrary"), 48),
        name="moe_gate_up",
    )(blk_e, nblk, nsub, xs, w_gate_up, w_gate_up, b_gate_up.reshape(e, 1, f2), b_gate_up.reshape(e, 1, f2))


def _moe2_kernel(be_ref, nb_ref, ns_ref, x_ref, w_ref, bias_ref, o_ref, wbf_ref):
    b = pl.program_id(1)
    valid = b < nb_ref[0]
    first = jnp.logical_or(b == 0, be_ref[b] != be_ref[jnp.maximum(b - 1, 0)])

    @pl.when(jnp.logical_and(valid, first))
    def _():
        wbf_ref[...] = w_ref[0].astype(BF16)

    @pl.when(valid)
    def _():
        def body(r, carry):
            rows = pl.ds(pl.multiple_of(r * MOE_SUB, MOE_SUB), MOE_SUB)
            o_ref[rows, :] = jnp.dot(x_ref[rows, :], wbf_ref[...], preferred_element_type=F32) + bias_ref[0]
            return carry

        def pad(r, carry):
            rows = pl.ds(pl.multiple_of(r * MOE_SUB, MOE_SUB), MOE_SUB)
            o_ref[rows, :] = jnp.zeros((MOE_SUB, o_ref.shape[1]), o_ref.dtype)
            return carry

        lax.fori_loop(0, ns_ref[b], body, 0)
        lax.fori_loop(ns_ref[b], x_ref.shape[0] // MOE_SUB, pad, 0)

    @pl.when(jnp.logical_not(valid))
    def _():
        o_ref[...] = jnp.zeros_like(o_ref)


def _moe_down(hdn, blk_e, nblk, nsub, w_down, b_down):
    rows_pad, f = hdn.shape
    e, _, d = w_down.shape
    tn = MOE2_TN
    nb = rows_pad // MOE_TM
    grid_spec = pltpu.PrefetchScalarGridSpec(
        num_scalar_prefetch=3,
        grid=(d // tn, nb),
        in_specs=[pl.BlockSpec((MOE_TM, f), lambda j, b, be, nbr, ns: (jnp.minimum(b, nbr[0] - 1), 0)),
                  pl.BlockSpec((1, f, tn), lambda j, b, be, nbr, ns: (be[b], 0, j)),
                  pl.BlockSpec((1, 1, tn), lambda j, b, be, nbr, ns: (be[b], 0, j))],
        out_specs=pl.BlockSpec((MOE_TM, tn), lambda j, b, be, nbr, ns: (b, j)),
        scratch_shapes=[pltpu.VMEM((f, tn), BF16)],
    )
    return pl.pallas_call(
        _moe2_kernel,
        grid_spec=grid_spec,
        out_shape=jax.ShapeDtypeStruct((rows_pad, d), F32),
        compiler_params=_cparams(("arbitrary", "arbitrary"), 48),
        name="moe_down",
    )(blk_e, nblk, nsub, hdn, w_down, b_down.reshape(e, 1, d))


def _combine_kernel(dest_ref, dest_next_ref, y_ref, gate_ref, h_ref, g_ref, o_ref, buf_ref, sem):
    i = pl.program_id(0)
    ntok = h_ref.shape[0]
    slot = i % 2

    def each_row(d_ref, s, fn):
        def body(t, carry):
            for k in range(TOP_K):
                fn(pltpu.make_async_copy(y_ref.at[pl.ds(d_ref[0, 0, t * TOP_K + k], 1)],
                                         buf_ref.at[s, k, pl.ds(t, 1)], sem.at[s]))
            return carry

        lax.fori_loop(0, ntok, body, 0)

    @pl.when(i == 0)
    def _():
        each_row(dest_ref, 0, lambda cp: cp.start())

    @pl.when(i + 1 < pl.num_programs(0))
    def _():
        each_row(dest_next_ref, 1 - slot, lambda cp: cp.start())

    each_row(dest_ref, slot, lambda cp: cp.wait())
    gate = gate_ref[...]
    acc = h_ref[...]
    for k in range(TOP_K):
        acc = acc + gate[:, k:k + 1] * buf_ref[slot, k]
    o_ref[...] = acc * lax.rsqrt(jnp.mean(acc * acc, axis=-1, keepdims=True) + RMS_EPS) * g_ref[...]


def _combine(yb, dest, gate, h, g_final):
    n, d = h.shape
    tok = COMBINE_TOK
    nsteps = n // tok
    dest3 = dest.reshape(nsteps, 1, tok * TOP_K)
    return pl.pallas_call(
        _combine_kernel,
        grid=(nsteps,),
        in_specs=[pl.BlockSpec((1, 1, tok * TOP_K), lambda i: (i, 0, 0), memory_space=pltpu.SMEM),
                  pl.BlockSpec((1, 1, tok * TOP_K), lambda i: (jnp.minimum(i + 1, nsteps - 1), 0, 0),
                               memory_space=pltpu.SMEM),
                  pl.BlockSpec(memory_space=pl.ANY),
                  pl.BlockSpec((tok, TOP_K), lambda i: (i, 0)),
                  pl.BlockSpec((tok, d), lambda i: (i, 0)),
                  pl.BlockSpec((1, d), lambda i: (0, 0))],
        out_specs=pl.BlockSpec((tok, d), lambda i: (i, 0)),
        out_shape=jax.ShapeDtypeStruct((n, d), F32),
        scratch_shapes=[pltpu.VMEM((2, TOP_K, tok, d), F32), pltpu.SemaphoreType.DMA((2,))],
        compiler_params=_cparams(("arbitrary",), 32),
        name="moe_combine",
    )(dest3, dest3, yb, gate, h, g_final.reshape(1, d))


def _routing(logits, n_experts, rows_pad):
    top_val, top_idx = lax.top_k(logits, TOP_K)
    gate = jax.nn.softmax(top_val, axis=-1)
    flat_e = top_idx.reshape(-1).astype(jnp.int32)
    onehot = (flat_e[:, None] == jnp.arange(n_experts, dtype=jnp.int32)[None, :]).astype(jnp.int32)
    before = jnp.cumsum(onehot, axis=0) - onehot
    rank = jnp.sum(before * onehot, axis=1)
    counts = jnp.sum(onehot, axis=0)
    nblk_e = (counts + MOE_TM - 1) // MOE_TM
    blk_end = jnp.cumsum(nblk_e)
    pstart = (blk_end - nblk_e) * MOE_TM
    dest = jnp.sum(onehot * pstart[None, :], axis=1) + rank
    nblk = blk_end[-1]
    nb = rows_pad // MOE_TM
    bidx = jnp.minimum(jnp.arange(nb, dtype=jnp.int32), nblk - 1)
    blk_e = jnp.sum((bidx[:, None] >= blk_end[None, :]).astype(jnp.int32), axis=1)
    last_blk_row = jnp.where(nblk_e > 0, (blk_end - 1) * MOE_TM, -1)
    rows_in_blk = jnp.clip(counts[blk_e] - (bidx - (blk_end - nblk_e)[blk_e]) * MOE_TM, 0, MOE_TM)
    nsub = (rows_in_blk + MOE_SUB - 1) // MOE_SUB
    return (gate, dest.astype(jnp.int32), blk_e.astype(jnp.int32), nblk.reshape(1).astype(jnp.int32),
            last_blk_row.astype(jnp.int32), nsub.astype(jnp.int32))


def _rope_tables(pos, dk):
    half = dk // 2
    freq = ROPE_BASE ** (-jnp.arange(half, dtype=F32) / half)
    ang = pos.astype(F32)[:, None] * freq[None, :]
    cos, sin = jnp.cos(ang), jnp.sin(ang)
    return jnp.concatenate([cos, cos], axis=-1), jnp.concatenate([-sin, sin], axis=-1)


def kernel(x_prompt, x_sample, state_ret, cache_conv, norm_mix_g, w_in, w_ret_o, conv_dw_w, conv_dw_b, conv_ln_g, conv_ln_b, w_conv_o, w_o, norm_ffn_g, router_w, router_b, w_gate_up, b_gate_up, w_down, b_down, norm_final_g):
    bp, tp, d = x_prompt.shape
    bs, ts, _ = x_sample.shape
    depth, _, heads, dk, dv = state_ret.shape
    ch = cache_conv.shape[-1]
    n_experts = router_w.shape[-1]
    np_rows, ns_rows = bp * tp, bs * ts
    n = np_rows + ns_rows
    qw, vw = heads * dk, heads * dv
    a_col = (2 * qw + 2 * vw) // ch
    gate_col = (2 * qw + 2 * vw + 2 * ch) // d
    rows_pad = (n * TOP_K // MOE_TM + n_experts) * MOE_TM

    cos_p, sin_p = _rope_tables(jnp.arange(tp, dtype=jnp.int32), dk)
    cos_s, sin_s = _rope_tables(PAST_LEN + jnp.arange(ts, dtype=jnp.int32), dk)

    assert depth == 1, "the final norm is fused into the layer's last stage"
    xa, xb = x_prompt.reshape(np_rows, d), x_sample.reshape(ns_rows, d)
    xn = _rmsnorm_bf16(xa, xb, norm_mix_g[0])
    u = _input_projection(xn, w_in[0])

    zero_ret = jnp.zeros((bp,) + state_ret.shape[2:], F32)
    zero_conv = jnp.zeros((bp,) + cache_conv.shape[2:], F32)
    r_p, s_p = _retention(u, 0, bp, tp, zero_ret, cos_p, sin_p)
    r_s, s_s = _retention(u, np_rows, bs, ts, state_ret[0], cos_s, sin_s)
    c_p, cc_p = _conv_module(u, 0, bp, tp, zero_conv, conv_dw_w[0], conv_dw_b[0], conv_ln_g[0], conv_ln_b[0], a_col)
    c_s, cc_s = _conv_module(u, np_rows, bs, ts, cache_conv[0], conv_dw_w[0], conv_dw_b[0], conv_ln_g[0],
                             conv_ln_b[0], a_col)
    r = jnp.concatenate([r_p, r_s], axis=0)
    cv = jnp.concatenate([c_p, c_s], axis=0)

    e_pad = -(-n_experts // LANES) * LANES
    rw = jnp.pad(router_w[0], ((0, 0), (0, e_pad - n_experts))).astype(BF16)
    rb = jnp.pad(router_b[0], (0, e_pad - n_experts)).reshape(1, e_pad)
    hmid, hnp, logits = _merge(r, cv, u, xa, xb, w_ret_o[0].astype(BF16), w_conv_o[0].astype(BF16),
                               w_o[0].astype(BF16), norm_ffn_g[0], rw, rb, gate_col)

    gate, dest, blk_e, nblk, last_blk_row, nsub = _routing(logits[:, :n_experts], n_experts, rows_pad)
    xs = _dispatch(hnp, dest, last_blk_row, nblk, rows_pad)
    hdn = _moe_gate_up(xs, blk_e, nblk, nsub, w_gate_up[0], b_gate_up[0])
    yb = _moe_down(hdn, blk_e, nblk, nsub, w_down[0], b_down[0])
    y = _combine(yb, dest, gate, hmid, norm_final_g)
    y_prompt = y[:np_rows].reshape(bp, tp, d)
    y_sample = y[np_rows:].reshape(bs, ts, d)
    return (y_prompt, y_sample, s_p[None], cc_p[None], s_s[None], cc_s[None])
```

```python
import functools
import math

import jax
import jax.numpy as jnp
from jax import lax
from jax.experimental import pallas as pl
from jax.experimental.pallas import tpu as pltpu

TOP_K = 4
PAST_LEN = 1024
ROPE_BASE = 10000.0
RMS_EPS = 1e-6
LN_EPS = 1e-5
SWIGLU_LIMIT = 7.0
SWIGLU_ALPHA = 1.702

F32 = jnp.float32
BF16 = jnp.bfloat16
U32 = jnp.uint32

V7X_VMEM_BYTES = 64 * 1024 * 1024
LANES = 128

NORM_ROWS = 512
PROJ_TM = 1024
PROJ_TN = 1024
PROJ_SUB = 256
RET_CHUNK_MAX = 256
CONV_TT_MAX = 256
CONV_RB = 128
MERGE_TM = 256
MOE_TM = 512
MOE1_TF = 1024
MOE2_TN = 2048
MOE_SUB = 256
DISPATCH_TOK = 256
COMBINE_TOK = 128


def _cparams(sem, vmem_mb):
    return pltpu.CompilerParams(dimension_semantics=sem, vmem_limit_bytes=vmem_mb * 1024 * 1024)


def _rmsnorm_kernel(xa_ref, xb_ref, g_ref, o_ref, *, a_steps):
    def norm(x_ref):
        x = x_ref[...]
        ms = jnp.mean(x * x, axis=-1, keepdims=True)
        o_ref[...] = (x * lax.rsqrt(ms + RMS_EPS) * g_ref[...]).astype(o_ref.dtype)

    @pl.when(pl.program_id(0) < a_steps)
    def _():
        norm(xa_ref)

    @pl.when(pl.program_id(0) >= a_steps)
    def _():
        norm(xb_ref)


def _rmsnorm_bf16(xa, xb, g):
    (na, d), nb = xa.shape, xb.shape[0]
    a_steps = na // NORM_ROWS
    return pl.pallas_call(
        functools.partial(_rmsnorm_kernel, a_steps=a_steps),
        grid=((na + nb) // NORM_ROWS,),
        in_specs=[pl.BlockSpec((NORM_ROWS, d), lambda i: (jnp.minimum(i, a_steps - 1), 0)),
                  pl.BlockSpec((NORM_ROWS, d), lambda i: (jnp.maximum(i - a_steps, 0), 0)),
                  pl.BlockSpec((1, d), lambda i: (0, 0))],
        out_specs=pl.BlockSpec((NORM_ROWS, d), lambda i: (i, 0)),
        out_shape=jax.ShapeDtypeStruct((na + nb, d), BF16),
        compiler_params=_cparams(("arbitrary",), 32),
        name="rmsnorm_bf16",
    )(xa, xb, g.reshape(1, d))


def _inproj_kernel(x_ref, w_ref, o_ref, wbf_ref):
    @pl.when(pl.program_id(1) == 0)
    def _():
        wbf_ref[...] = w_ref[...].astype(BF16)

    def body(r, carry):
        rows = pl.ds(pl.multiple_of(r * PROJ_SUB, PROJ_SUB), PROJ_SUB)
        o_ref[rows, :] = jnp.dot(x_ref[rows, :], wbf_ref[...], preferred_element_type=F32)
        return carry

    lax.fori_loop(0, PROJ_TM // PROJ_SUB, body, 0)


def _input_projection(xn, w):
    n, d = xn.shape
    width = w.shape[1]
    return pl.pallas_call(
        _inproj_kernel,
        grid=(width // PROJ_TN, n // PROJ_TM),
        in_specs=[pl.BlockSpec((PROJ_TM, d), lambda j, i: (i, 0)),
                  pl.BlockSpec((d, PROJ_TN), lambda j, i: (0, j))],
        out_specs=pl.BlockSpec((PROJ_TM, PROJ_TN), lambda j, i: (i, j)),
        out_shape=jax.ShapeDtypeStruct((n, width), F32),
        scratch_shapes=[pltpu.VMEM((d, PROJ_TN), BF16)],
        compiler_params=_cparams(("arbitrary", "arbitrary"), 48),
        name="input_projection",
    )(xn, w)


def _retention_kernel(q_ref, k_ref, v_ref, g_ref, cos_ref, sin_ref, s0_ref, r_ref, sout_ref, state_ref,
                      *, chunk, heads, dk, dv):
    c = pl.program_id(1)

    @pl.when(c == 0)
    def _():
        state_ref[...] = s0_ref[0]

    cos = cos_ref[...]
    sin = sin_ref[...]
    li = lax.broadcasted_iota(jnp.int32, (chunk, chunk), 0)
    mi = lax.broadcasted_iota(jnp.int32, (chunk, chunk), 1)
    rel = (li - mi).astype(F32)
    row = lax.broadcasted_iota(jnp.int32, (chunk, 1), 0).astype(F32)
    for h in range(heads):
        lg = math.log1p(-(2.0 ** (-5.0 - h)))
        q = q_ref[:, h * dk:(h + 1) * dk]
        k = k_ref[:, h * dk:(h + 1) * dk]
        v = v_ref[:, h * dv:(h + 1) * dv].astype(BF16)
        qr = (q * cos + pltpu.roll(q, dk // 2, 1) * sin).astype(BF16)
        kr = (k * cos + pltpu.roll(k, dk // 2, 1) * sin) * (dk ** -0.5)
        decay = jnp.where(rel >= 0, jnp.exp(lg * jnp.maximum(rel, 0.0)), 0.0)
        scores = lax.dot_general(qr, kr.astype(BF16), (((1,), (1,)), ((), ())),
                                 preferred_element_type=F32) * decay
        intra = jnp.dot(scores.astype(BF16), v, preferred_element_type=F32)
        st = state_ref[h]
        cross = jnp.dot(qr, st.astype(BF16), preferred_element_type=F32) * jnp.exp(lg * (row + 1.0))
        o = intra + cross
        kdec = (kr * jnp.exp(lg * (chunk - 1.0 - row))).astype(BF16)
        state_ref[h] = math.exp(lg * chunk) * st + lax.dot_general(
            kdec, v, (((0,), (0,)), ((), ())), preferred_element_type=F32)
        mu = jnp.mean(o, axis=-1, keepdims=True)
        var = jnp.mean(jnp.square(o - mu), axis=-1, keepdims=True)
        on = (o - mu) * lax.rsqrt(var + LN_EPS)
        g = g_ref[:, h * dv:(h + 1) * dv]
        r_ref[:, h * dv:(h + 1) * dv] = (g * jax.nn.sigmoid(g) * on).astype(r_ref.dtype)

    @pl.when(c == pl.num_programs(1) - 1)
    def _():
        sout_ref[0] = state_ref[...]


def _retention(u, row0, batch, seq, state0, cos, sin):
    _, heads, dk, dv = state0.shape
    chunk = min(seq, RET_CHUNK_MAX)
    nc = seq // chunk
    rb0 = row0 // chunk
    qw, vw = heads * dk, heads * dv

    def rows(b, c):
        return rb0 + b * nc + c

    kern = functools.partial(_retention_kernel, chunk=chunk, heads=heads, dk=dk, dv=dv)
    return pl.pallas_call(
        kern,
        grid=(batch, nc),
        in_specs=[pl.BlockSpec((chunk, qw), lambda b, c: (rows(b, c), 0)),
                  pl.BlockSpec((chunk, qw), lambda b, c: (rows(b, c), 1)),
                  pl.BlockSpec((chunk, vw), lambda b, c: (rows(b, c), (2 * qw) // vw)),
                  pl.BlockSpec((chunk, vw), lambda b, c: (rows(b, c), (2 * qw) // vw + 1)),
                  pl.BlockSpec((chunk, dk), lambda b, c: (c, 0)),
                  pl.BlockSpec((chunk, dk), lambda b, c: (c, 0)),
                  pl.BlockSpec((1, heads, dk, dv), lambda b, c: (b, 0, 0, 0))],
        out_specs=[pl.BlockSpec((chunk, vw), lambda b, c: (b * nc + c, 0)),
                   pl.BlockSpec((1, heads, dk, dv), lambda b, c: (b, 0, 0, 0))],
        out_shape=[jax.ShapeDtypeStruct((batch * seq, vw), BF16),
                   jax.ShapeDtypeStruct(state0.shape, F32)],
        scratch_shapes=[pltpu.VMEM((heads, dk, dv), F32)],
        compiler_params=_cparams(("arbitrary", "arbitrary"), 48),
        name="retention",
    )(u, u, u, u, cos, sin, state0)


def _conv_kernel(a_ref, b_ref, hist_ref, w_ref, bias_ref, lng_ref, lnb_ref, o_ref, cache_ref, s_ref, y_ref,
                 *, tt, width, off):
    t = pl.program_id(1)
    hl = width - 1
    ch = a_ref.shape[1]

    @pl.when(t == 0)
    def _():
        s_ref[off - hl:off, :] = hist_ref[0]

    @pl.when(t > 0)
    def _():
        s_ref[off - hl:off, :] = s_ref[off + tt - hl:off + tt, :]

    s_ref[off:off + tt, :] = a_ref[...] * jax.nn.sigmoid(b_ref[...])
    cache_ref[0] = s_ref[off + tt - hl:off + tt, :]

    rb = min(CONV_RB, tt)
    for cc in range(ch // LANES):
        cols = slice(cc * LANES, (cc + 1) * LANES)
        for r in range(tt // rb):
            acc = jnp.broadcast_to(bias_ref[:, cols], (rb, LANES))
            for j in range(width):
                st = off - hl + j + r * rb
                acc = acc + w_ref[j:j + 1, cols] * s_ref[st:st + rb, cols]
            y_ref[r * rb:(r + 1) * rb, cols] = acc

    y = y_ref[...]
    mu = jnp.mean(y, axis=-1, keepdims=True)
    var = jnp.mean(jnp.square(y - mu), axis=-1, keepdims=True)
    yn = (y - mu) * lax.rsqrt(var + LN_EPS) * lng_ref[...] + lnb_ref[...]
    o_ref[...] = (yn * jax.nn.sigmoid(yn)).astype(o_ref.dtype)


def _conv_module(u, row0, batch, seq, hist, conv_w, conv_b, ln_g, ln_b, a_col):
    width, ch = conv_w.shape
    tt = min(seq, CONV_TT_MAX)
    nt = seq // tt
    rb0 = row0 // tt
    off = 32
    assert width - 1 <= off and tt >= width - 1
    kern = functools.partial(_conv_kernel, tt=tt, width=width, off=off)
    vec = pl.BlockSpec((1, ch), lambda b, t: (0, 0))
    return pl.pallas_call(
        kern,
        grid=(batch, nt),
        in_specs=[pl.BlockSpec((tt, ch), lambda b, t: (rb0 + b * nt + t, a_col)),
                  pl.BlockSpec((tt, ch), lambda b, t: (rb0 + b * nt + t, a_col + 1)),
                  pl.BlockSpec((1, width - 1, ch), lambda b, t: (b, 0, 0)),
                  pl.BlockSpec((width, ch), lambda b, t: (0, 0)),
                  vec, vec, vec],
        out_specs=[pl.BlockSpec((tt, ch), lambda b, t: (b * nt + t, 0)),
                   pl.BlockSpec((1, width - 1, ch), lambda b, t: (b, 0, 0))],
        out_shape=[jax.ShapeDtypeStruct((batch * seq, ch), BF16),
                   jax.ShapeDtypeStruct((batch, width - 1, ch), F32)],
        scratch_shapes=[pltpu.VMEM((off + tt, ch), F32), pltpu.VMEM((tt, ch), F32)],
        compiler_params=_cparams(("arbitrary", "arbitrary"), 32),
        name="conv_module",
    )(u, u, hist, conv_w, conv_b.reshape(1, ch), ln_g.reshape(1, ch), ln_b.reshape(1, ch))


def _merge_kernel(r_ref, cv_ref, glr_ref, glc_ref, xa_ref, xb_ref, wr_ref, wc_ref, wo_ref, gf_ref, rw_ref, rb_ref,
                  h_ref, hnp_ref, lg_ref, *, a_steps):
    br = jnp.dot(r_ref[...], wr_ref[...], preferred_element_type=F32)
    bc = jnp.dot(cv_ref[...], wc_ref[...], preferred_element_type=F32)
    m = jax.nn.sigmoid(glr_ref[...]) * br + jax.nn.sigmoid(glc_ref[...]) * bc
    mo = jnp.dot(m.astype(BF16), wo_ref[...], preferred_element_type=F32)

    @pl.when(pl.program_id(0) < a_steps)
    def _():
        h_ref[...] = xa_ref[...] + mo

    @pl.when(pl.program_id(0) >= a_steps)
    def _():
        h_ref[...] = xb_ref[...] + mo

    h = h_ref[...]
    hn = h * lax.rsqrt(jnp.mean(h * h, axis=-1, keepdims=True) + RMS_EPS) * gf_ref[...]
    half = hn.shape[1] // 2
    a = hn[:, :half].astype(BF16)
    b = hn[:, half:].astype(BF16)
    lg_ref[...] = (jnp.dot(a, rw_ref[:half, :], preferred_element_type=F32)
                   + jnp.dot(b, rw_ref[half:, :], preferred_element_type=F32) + rb_ref[...])
    au = lax.bitcast_convert_type(a.astype(F32), U32)
    bu = lax.bitcast_convert_type(b.astype(F32), U32)
    hnp_ref[...] = au | (bu >> 16)


def _merge(r, cv, u, xa, xb, w_ret_o, w_conv_o, w_o, g_ffn, router_w, router_b, gate_col):
    (na, d), n = xa.shape, r.shape[0]
    ch = cv.shape[1]
    e_pad = router_w.shape[1]
    tm = MERGE_TM
    a_steps = na // tm

    def const(shape):
        return pl.BlockSpec(shape, lambda i: (0,) * len(shape), pipeline_mode=pl.Buffered(1))

    return pl.pallas_call(
        functools.partial(_merge_kernel, a_steps=a_steps),
        grid=(n // tm,),
        in_specs=[pl.BlockSpec((tm, r.shape[1]), lambda i: (i, 0)),
                  pl.BlockSpec((tm, ch), lambda i: (i, 0)),
                  pl.BlockSpec((tm, d), lambda i: (i, gate_col)),
                  pl.BlockSpec((tm, d), lambda i: (i, gate_col + 1)),
                  pl.BlockSpec((tm, d), lambda i: (jnp.minimum(i, a_steps - 1), 0)),
                  pl.BlockSpec((tm, d), lambda i: (jnp.maximum(i - a_steps, 0), 0)),
                  const(w_ret_o.shape), const(w_conv_o.shape), const(w_o.shape),
                  const((1, d)), const(router_w.shape), const((1, e_pad))],
        out_specs=[pl.BlockSpec((tm, d), lambda i: (i, 0)),
                   pl.BlockSpec((tm, d // 2), lambda i: (i, 0)),
                   pl.BlockSpec((tm, e_pad), lambda i: (i, 0))],
        out_shape=[jax.ShapeDtypeStruct((n, d), F32),
                   jax.ShapeDtypeStruct((n, d // 2), U32),
                   jax.ShapeDtypeStruct((n, e_pad), F32)],
        compiler_params=_cparams(("arbitrary",), 56),
        name="merge_router",
    )(r, cv, u, u, xa, xb, w_ret_o, w_conv_o, w_o, g_ffn.reshape(1, d), router_w, router_b)


def _dispatch_kernel(last_ref, nblk_ref, dest_ref, src_ref, out_ref, zero_ref, sem, zsem):
    ntok = src_ref.shape[0]
    nb_total = out_ref.shape[0] // MOE_TM

    def zero_block(row0):
        return pltpu.make_async_copy(zero_ref, out_ref.at[pl.ds(pl.multiple_of(row0, MOE_TM), MOE_TM)], zsem)

    @pl.when(pl.program_id(0) == 0)
    def _():
        zero_ref[...] = jnp.zeros_like(zero_ref)

        def each_block(fn):
            def expert(e, carry):
                @pl.when(last_ref[e] >= 0)
                def _():
                    fn(zero_block(last_ref[e]))
                return carry

            def tail(b, carry):
                fn(zero_block(b * MOE_TM))
                return carry

            lax.fori_loop(0, last_ref.shape[0], expert, 0)
            lax.fori_loop(nblk_ref[0], nb_total, tail, 0)

        each_block(lambda cp: cp.start())
        each_block(lambda cp: cp.wait())

    def each_row(fn):
        def body(t, carry):
            for k in range(TOP_K):
                fn(pltpu.make_async_copy(src_ref.at[pl.ds(t, 1)],
                                         out_ref.at[pl.ds(dest_ref[0, 0, t * TOP_K + k], 1)], sem))
            return carry

        lax.fori_loop(0, ntok, body, 0)

    each_row(lambda cp: cp.start())
    each_row(lambda cp: cp.wait())


def _dispatch(hnp, dest, last_blk_row, nblk, rows_pad):
    n, dw = hnp.shape
    tok = DISPATCH_TOK
    nsteps = n // tok
    grid_spec = pltpu.PrefetchScalarGridSpec(
        num_scalar_prefetch=2,
        grid=(nsteps,),
        in_specs=[pl.BlockSpec((1, 1, tok * TOP_K), lambda i, *_: (i, 0, 0), memory_space=pltpu.SMEM),
                  pl.BlockSpec((tok, dw), lambda i, *_: (i, 0))],
        out_specs=pl.BlockSpec(memory_space=pl.ANY),
        scratch_shapes=[pltpu.VMEM((MOE_TM, dw), U32), pltpu.SemaphoreType.DMA(()), pltpu.SemaphoreType.DMA(())],
    )
    return pl.pallas_call(
        _dispatch_kernel,
        grid_spec=grid_spec,
        out_shape=jax.ShapeDtypeStruct((rows_pad, dw), U32),
        compiler_params=_cparams(("arbitrary",), 32),
        name="moe_dispatch",
    )(last_blk_row, nblk, dest.reshape(nsteps, 1, tok * TOP_K), hnp)


def _unpack_bf16_pair(w):
    hi = lax.bitcast_convert_type(w & jnp.uint32(0xFFFF0000), F32).astype(BF16)
    lo = lax.bitcast_convert_type(w << 16, F32).astype(BF16)
    return hi, lo


def _moe1_kernel(be_ref, nb_ref, x_ref, wg_ref, wu_ref, bg_ref, bu_ref, o_ref, wbf_ref):
    b = pl.program_id(1)
    tf = wg_ref.shape[2]
    valid = b < nb_ref[0]
    first = jnp.logical_or(b == 0, be_ref[b] != be_ref[jnp.maximum(b - 1, 0)])

    @pl.when(jnp.logical_and(valid, first))
    def _():
        wbf_ref[:, :tf] = wg_ref[0].astype(BF16)
        wbf_ref[:, tf:] = wu_ref[0].astype(BF16)

    @pl.when(valid)
    def _():
        half = x_ref.shape[1]

        def body(r, carry):
            rows = pl.ds(pl.multiple_of(r * MOE_SUB, MOE_SUB), MOE_SUB)
            hi, lo = _unpack_bf16_pair(x_ref[rows, :])
            gu = (jnp.dot(hi, wbf_ref[:half, :], preferred_element_type=F32)
                  + jnp.dot(lo, wbf_ref[half:, :], preferred_element_type=F32))
            g = jnp.minimum(gu[:, :tf] + bg_ref[0], SWIGLU_LIMIT)
            u = jnp.clip(gu[:, tf:] + bu_ref[0], -SWIGLU_LIMIT, SWIGLU_LIMIT)
            o_ref[rows, :] = ((u + 1.0) * (g * jax.nn.sigmoid(SWIGLU_ALPHA * g))).astype(o_ref.dtype)
            return carry

        lax.fori_loop(0, x_ref.shape[0] // MOE_SUB, body, 0)

    @pl.when(jnp.logical_not(valid))
    def _():
        o_ref[...] = jnp.zeros_like(o_ref)


def _moe_gate_up(xs, blk_e, nblk, w_gate_up, b_gate_up):
    rows_pad, dw = xs.shape
    e, d, f2 = w_gate_up.shape
    f = f2 // 2
    tf = MOE1_TF
    nb = rows_pad // MOE_TM
    nj = f // tf

    def xrow(j, b, be, nbr):
        return (jnp.minimum(b, nbr[0] - 1), 0)

    grid_spec = pltpu.PrefetchScalarGridSpec(
        num_scalar_prefetch=2,
        grid=(nj, nb),
        in_specs=[pl.BlockSpec((MOE_TM, dw), xrow),
                  pl.BlockSpec((1, d, tf), lambda j, b, be, nbr: (be[b], 0, j)),
                  pl.BlockSpec((1, d, tf), lambda j, b, be, nbr: (be[b], 0, nj + j)),
                  pl.BlockSpec((1, 1, tf), lambda j, b, be, nbr: (be[b], 0, j)),
                  pl.BlockSpec((1, 1, tf), lambda j, b, be, nbr: (be[b], 0, nj + j))],
        out_specs=pl.BlockSpec((MOE_TM, tf), lambda j, b, be, nbr: (b, j)),
        scratch_shapes=[pltpu.VMEM((d, 2 * tf), BF16)],
    )
    return pl.pallas_call(
        _moe1_kernel,
        grid_spec=grid_spec,
        out_shape=jax.ShapeDtypeStruct((rows_pad, f), BF16),
        compiler_params=_cparams(("arbitrary", "arbitrary"), 56),
        name="moe_gate_up",
    )(blk_e, nblk, xs, w_gate_up, w_gate_up, b_gate_up.reshape(e, 1, f2), b_gate_up.reshape(e, 1, f2))


def _moe2_kernel(be_ref, nb_ref, x_ref, w_ref, bias_ref, o_ref, wbf_ref):
    b = pl.program_id(1)
    valid = b < nb_ref[0]
    first = jnp.logical_or(b == 0, be_ref[b] != be_ref[jnp.maximum(b - 1, 0)])

    @pl.when(jnp.logical_and(valid, first))
    def _():
        wbf_ref[...] = w_ref[0].astype(BF16)

    @pl.when(valid)
    def _():
        def body(r, carry):
            rows = pl.ds(pl.multiple_of(r * MOE_SUB, MOE_SUB), MOE_SUB)
            o_ref[rows, :] = jnp.dot(x_ref[rows, :], wbf_ref[...], preferred_element_type=F32) + bias_ref[0]
            return carry

        lax.fori_loop(0, x_ref.shape[0] // MOE_SUB, body, 0)

    @pl.when(jnp.logical_not(valid))
    def _():
        o_ref[...] = jnp.zeros_like(o_ref)


def _moe_down(hdn, blk_e, nblk, w_down, b_down):
    rows_pad, f = hdn.shape
    e, _, d = w_down.shape
    tn = MOE2_TN
    nb = rows_pad // MOE_TM
    grid_spec = pltpu.PrefetchScalarGridSpec(
        num_scalar_prefetch=2,
        grid=(d // tn, nb),
        in_specs=[pl.BlockSpec((MOE_TM, f), lambda j, b, be, nbr: (jnp.minimum(b, nbr[0] - 1), 0)),
                  pl.BlockSpec((1, f, tn), lambda j, b, be, nbr: (be[b], 0, j)),
                  pl.BlockSpec((1, 1, tn), lambda j, b, be, nbr: (be[b], 0, j))],
        out_specs=pl.BlockSpec((MOE_TM, tn), lambda j, b, be, nbr: (b, j)),
        scratch_shapes=[pltpu.VMEM((f, tn), BF16)],
    )
    return pl.pallas_call(
        _moe2_kernel,
        grid_spec=grid_spec,
        out_shape=jax.ShapeDtypeStruct((rows_pad, d), F32),
        compiler_params=_cparams(("arbitrary", "arbitrary"), 58),
        name="moe_down",
    )(blk_e, nblk, hdn, w_down, b_down.reshape(e, 1, d))


def _combine_kernel(dest_ref, dest_next_ref, y_ref, gate_ref, h_ref, g_ref, o_ref, buf_ref, sem):
    i = pl.program_id(0)
    ntok = h_ref.shape[0]
    slot = i % 2

    def each_row(d_ref, s, fn):
        def body(t, carry):
            for k in range(TOP_K):
                fn(pltpu.make_async_copy(y_ref.at[pl.ds(d_ref[0, 0, t * TOP_K + k], 1)],
                                         buf_ref.at[s, k, pl.ds(t, 1)], sem.at[s]))
            return carry

        lax.fori_loop(0, ntok, body, 0)

    @pl.when(i == 0)
    def _():
        each_row(dest_ref, 0, lambda cp: cp.start())

    @pl.when(i + 1 < pl.num_programs(0))
    def _():
        each_row(dest_next_ref, 1 - slot, lambda cp: cp.start())

    each_row(dest_ref, slot, lambda cp: cp.wait())
    gate = gate_ref[...]
    acc = h_ref[...]
    for k in range(TOP_K):
        acc = acc + gate[:, k:k + 1] * buf_ref[slot, k]
    o_ref[...] = acc * lax.rsqrt(jnp.mean(acc * acc, axis=-1, keepdims=True) + RMS_EPS) * g_ref[...]


def _combine(yb, dest, gate, h, g_final):
    n, d = h.shape
    tok = COMBINE_TOK
    nsteps = n // tok
    dest3 = dest.reshape(nsteps, 1, tok * TOP_K)
    return pl.pallas_call(
        _combine_kernel,
        grid=(nsteps,),
        in_specs=[pl.BlockSpec((1, 1, tok * TOP_K), lambda i: (i, 0, 0), memory_space=pltpu.SMEM),
                  pl.BlockSpec((1, 1, tok * TOP_K), lambda i: (jnp.minimum(i + 1, nsteps - 1), 0, 0),
                               memory_space=pltpu.SMEM),
                  pl.BlockSpec(memory_space=pl.ANY),
                  pl.BlockSpec((tok, TOP_K), lambda i: (i, 0)),
                  pl.BlockSpec((tok, d), lambda i: (i, 0)),
                  pl.BlockSpec((1, d), lambda i: (0, 0))],
        out_specs=pl.BlockSpec((tok, d), lambda i: (i, 0)),
        out_shape=jax.ShapeDtypeStruct((n, d), F32),
        scratch_shapes=[pltpu.VMEM((2, TOP_K, tok, d), F32), pltpu.SemaphoreType.DMA((2,))],
        compiler_params=_cparams(("arbitrary",), 32),
        name="moe_combine",
    )(dest3, dest3, yb, gate, h, g_final.reshape(1, d))


def _routing(logits, n_experts, rows_pad):
    top_val, top_idx = lax.top_k(logits, TOP_K)
    gate = jax.nn.softmax(top_val, axis=-1)
    flat_e = top_idx.reshape(-1).astype(jnp.int32)
    onehot = (flat_e[:, None] == jnp.arange(n_experts, dtype=jnp.int32)[None, :]).astype(jnp.int32)
    before = jnp.cumsum(onehot, axis=0) - onehot
    rank = jnp.sum(before * onehot, axis=1)
    counts = jnp.sum(onehot, axis=0)
    nblk_e = (counts + MOE_TM - 1) // MOE_TM
    blk_end = jnp.cumsum(nblk_e)
    pstart = (blk_end - nblk_e) * MOE_TM
    dest = jnp.sum(onehot * pstart[None, :], axis=1) + rank
    nblk = blk_end[-1]
    nb = rows_pad // MOE_TM
    bidx = jnp.minimum(jnp.arange(nb, dtype=jnp.int32), nblk - 1)
    blk_e = jnp.sum((bidx[:, None] >= blk_end[None, :]).astype(jnp.int32), axis=1)
    last_blk_row = jnp.where(nblk_e > 0, (blk_end - 1) * MOE_TM, -1)
    return (gate, dest.astype(jnp.int32), blk_e.astype(jnp.int32), nblk.reshape(1).astype(jnp.int32),
            last_blk_row.astype(jnp.int32))


def _rope_tables(pos, dk):
    half = dk // 2
    freq = ROPE_BASE ** (-jnp.arange(half, dtype=F32) / half)
    ang = pos.astype(F32)[:, None] * freq[None, :]
    cos, sin = jnp.cos(ang), jnp.sin(ang)
    return jnp.concatenate([cos, cos], axis=-1), jnp.concatenate([-sin, sin], axis=-1)


def kernel(x_prompt, x_sample, state_ret, cache_conv, norm_mix_g, w_in, w_ret_o, conv_dw_w, conv_dw_b, conv_ln_g, conv_ln_b, w_conv_o, w_o, norm_ffn_g, router_w, router_b, w_gate_up, b_gate_up, w_down, b_down, norm_final_g):
    bp, tp, d = x_prompt.shape
    bs, ts, _ = x_sample.shape
    depth, _, heads, dk, dv = state_ret.shape
    ch = cache_conv.shape[-1]
    n_experts = router_w.shape[-1]
    np_rows, ns_rows = bp * tp, bs * ts
    n = np_rows + ns_rows
    qw, vw = heads * dk, heads * dv
    a_col = (2 * qw + 2 * vw) // ch
    gate_col = (2 * qw + 2 * vw + 2 * ch) // d
    rows_pad = (n * TOP_K // MOE_TM + n_experts) * MOE_TM

    cos_p, sin_p = _rope_tables(jnp.arange(tp, dtype=jnp.int32), dk)
    cos_s, sin_s = _rope_tables(PAST_LEN + jnp.arange(ts, dtype=jnp.int32), dk)

    assert depth == 1, "the final norm is fused into the layer's last stage"
    xa, xb = x_prompt.reshape(np_rows, d), x_sample.reshape(ns_rows, d)
    xn = _rmsnorm_bf16(xa, xb, norm_mix_g[0])
    u = _input_projection(xn, w_in[0])

    zero_ret = jnp.zeros((bp,) + state_ret.shape[2:], F32)
    zero_conv = jnp.zeros((bp,) + cache_conv.shape[2:], F32)
    r_p, s_p = _retention(u, 0, bp, tp, zero_ret, cos_p, sin_p)
    r_s, s_s = _retention(u, np_rows, bs, ts, state_ret[0], cos_s, sin_s)
    c_p, cc_p = _conv_module(u, 0, bp, tp, zero_conv, conv_dw_w[0], conv_dw_b[0], conv_ln_g[0], conv_ln_b[0], a_col)
    c_s, cc_s = _conv_module(u, np_rows, bs, ts, cache_conv[0], conv_dw_w[0], conv_dw_b[0], conv_ln_g[0],
                             conv_ln_b[0], a_col)
    r = jnp.concatenate([r_p, r_s], axis=0)
    cv = jnp.concatenate([c_p, c_s], axis=0)

    e_pad = -(-n_experts // LANES) * LANES
    rw = jnp.pad(router_w[0], ((0, 0), (0, e_pad - n_experts))).astype(BF16)
    rb = jnp.pad(router_b[0], (0, e_pad - n_experts)).reshape(1, e_pad)
    hmid, hnp, logits = _merge(r, cv, u, xa, xb, w_ret_o[0].astype(BF16), w_conv_o[0].astype(BF16),
                               w_o[0].astype(BF16), norm_ffn_g[0], rw, rb, gate_col)

    gate, dest, blk_e, nblk, last_blk_row = _routing(logits[:, :n_experts], n_experts, rows_pad)
    xs = _dispatch(hnp, dest, last_blk_row, nblk, rows_pad)
    hdn = _moe_gate_up(xs, blk_e, nblk, w_gate_up[0], b_gate_up[0])
    yb = _moe_down(hdn, blk_e, nblk, w_down[0], b_down[0])
    y = _combine(yb, dest, gate, hmid, norm_final_g)
    y_prompt = y[:np_rows].reshape(bp, tp, d)
    y_sample = y[np_rows:].reshape(bs, ts, d)
    return (y_prompt, y_sample, s_p[None], cc_p[None], s_s[None], cc_s[None])
```
